```python
import math
import jax, jax.numpy as jnp
from jax import lax
import numpy as np

D_MODEL = 1024
BATCH = 8
SEQ = 4096
DEPTH = 1

ATTN_HEAD_DIM = 64
ATTN_PATTERNS = ((128, 1), (512, 4), (2048, 16))
HEADS_PER_PATTERN = 8
N_ATTN_HEADS = HEADS_PER_PATTERN * len(ATTN_PATTERNS)
ATTN_QKV = N_ATTN_HEADS * ATTN_HEAD_DIM
ATTN_OUT = HEADS_PER_PATTERN * ATTN_HEAD_DIM
ATTN_BLOCK = 128
ALIBI_MAX_EXP = 8.0
SSD_EXPAND = 2
SSD_INNER = SSD_EXPAND * D_MODEL
SSD_HEAD_DIM = 64
SSD_HEADS = SSD_INNER // SSD_HEAD_DIM
SSD_STATE = 128
SSD_GROUPS = 4
SSD_CONV = 4
SSD_CHUNK = 128
SSD_CONV_DIM = SSD_INNER + 2 * SSD_GROUPS * SSD_STATE
D_FF = 2816
EPS = 1e-6
IN_COLS = 3 * ATTN_QKV + SSD_INNER + SSD_CONV_DIM + SSD_HEADS + 2 * D_MODEL

kernel_name = "hybrid_dilated_attn_ssd_macaron"


def rmsnorm(x, g):
    x32 = x.astype(jnp.float32)
    y = x32 * lax.rsqrt(jnp.mean(x32 * x32, axis=-1, keepdims=True) + EPS)
    return (y * g.astype(jnp.float32)).astype(x.dtype)


def swiglu(h, w_gate, w_up, w_down):
    return (jax.nn.silu(h @ w_gate) * (h @ w_up)) @ w_down


def alibi_slopes(n):
    return jnp.exp2(-ALIBI_MAX_EXP * jnp.arange(1, n + 1, dtype=jnp.float32) / n)


def dilated_window_attention(q, k, v, slopes, window, dilation):
    b, S, H, hd = q.shape
    L = S // dilation
    n_back = window // dilation
    nb = -(-L // ATTN_BLOCK)
    Lp = nb * ATTN_BLOCK

    def to_blocks(t):
        t = t.reshape(b, L, dilation, H, hd).transpose(0, 2, 1, 3, 4)
        t = jnp.pad(t, ((0, 0), (0, 0), (0, Lp - L), (0, 0), (0, 0)))
        return t.reshape(b, dilation, nb, ATTN_BLOCK, H, hd)

    def with_prev(t):
        prev = jnp.pad(t, ((0, 0), (0, 0), (1, 0), (0, 0), (0, 0), (0, 0)))[:, :, :-1]
        return jnp.concatenate([prev, t], axis=3)

    qb = to_blocks(q)
    kk = with_prev(to_blocks(k))
    vv = with_prev(to_blocks(v))
    scale = 1.0 / math.sqrt(hd)
    logits = jnp.einsum('bdnqhe,bdnkhe->bdnhqk', qb, kk).astype(jnp.float32) * scale

    a_idx = jnp.arange(ATTN_BLOCK)[:, None]
    c_idx = jnp.arange(2 * ATTN_BLOCK)[None, :]
    rel = ATTN_BLOCK + a_idx - c_idx
    band = (rel >= 0) & (rel <= n_back)
    key_pos = jnp.arange(nb)[:, None] * ATTN_BLOCK + jnp.arange(2 * ATTN_BLOCK)[None, :] - ATTN_BLOCK
    mask = band[None] & (key_pos >= 0)[:, None, :]
    bias = -slopes[:, None, None] * (rel * dilation).astype(jnp.float32)[None]
    logits = jnp.where(mask[None, None, :, None], logits + bias[None, None, None], -jnp.inf)

    m = jnp.max(logits, axis=-1, keepdims=True)
    p = jnp.exp(logits - m)
    l = jnp.sum(p, axis=-1, keepdims=True)
    o = jnp.einsum('bdnhqk,bdnkhe->bdnqhe', p / l, vv.astype(jnp.float32))
    lse = (m + jnp.log(l))[..., 0]

    o = o.reshape(b, dilation, Lp, H, hd)[:, :, :L].transpose(0, 2, 1, 3, 4).reshape(b, S, H, hd)
    lse = lse.transpose(0, 1, 2, 4, 3).reshape(b, dilation, Lp, H)[:, :, :L]
    lse = lse.transpose(0, 2, 1, 3).reshape(b, S, H)
    return o, lse


def attention_branch(q, k, v, q_gain, k_gain):
    b, S, _ = q.shape
    q = rmsnorm(q.reshape(b, S, N_ATTN_HEADS, ATTN_HEAD_DIM), q_gain)
    k = rmsnorm(k.reshape(b, S, N_ATTN_HEADS, ATTN_HEAD_DIM), k_gain)
    v = v.reshape(b, S, N_ATTN_HEADS, ATTN_HEAD_DIM)
    slopes = alibi_slopes(N_ATTN_HEADS)
    outs, lses = [], []
    for g, (window, dilation) in enumerate(ATTN_PATTERNS):
        hs = slice(g * HEADS_PER_PATTERN, (g + 1) * HEADS_PER_PATTERN)
        o, lse = dilated_window_attention(q[:, :, hs], k[:, :, hs], v[:, :, hs],
                                          slopes[hs], window, dilation)
        outs.append(o)
        lses.append(lse)
    w = jax.nn.softmax(jnp.stack(lses, axis=0), axis=0)
    o = jnp.sum(w[..., None] * jnp.stack(outs, axis=0), axis=0)
    return o.reshape(b, S, ATTN_OUT).astype(q.dtype)


def causal_depthwise_conv(u, w, bias):
    C = u.shape[-1]
    y = lax.conv_general_dilated(u, w[:, None, :], window_strides=(1,),
                                 padding=[(SSD_CONV - 1, 0)],
                                 dimension_numbers=('NWC', 'WIO', 'NWC'),
                                 feature_group_count=C)
    return y + bias


def ssd_chunked(x, a, Bm, Cm):
    b, S, H, P = x.shape
    G, N = Bm.shape[2], Bm.shape[3]
    J = H // G
    nc = S // SSD_CHUNK
    Q = SSD_CHUNK
    x = x.reshape(b, nc, Q, G, J, P)
    a = a.reshape(b, nc, Q, G, J).transpose(0, 1, 3, 4, 2)
    Bm = Bm.reshape(b, nc, Q, G, N)
    Cm = Cm.reshape(b, nc, Q, G, N)
    a_cs = jnp.cumsum(a, axis=-1)
    diff = a_cs[..., :, None] - a_cs[..., None, :]
    causal = jnp.tril(jnp.ones((Q, Q), dtype=bool))
    decay = jnp.exp(jnp.where(causal, diff, -jnp.inf))
    CB = jnp.einsum('bclgn,bcsgn->bcgls', Cm, Bm)
    y_diag = jnp.einsum('bcgjls,bcsgjp->bclgjp', CB[:, :, :, None] * decay, x)
    decay_states = jnp.exp(a_cs[..., -1:] - a_cs)
    states = jnp.einsum('bclgn,bcgjl,bclgjp->bcgjpn', Bm, decay_states, x)
    chunk_decay = jnp.exp(a_cs[..., -1])

    def step(carry, inp):
        st, dec = inp
        return carry * dec[..., None, None] + st, carry

    init = jnp.zeros((b, G, J, P, N), dtype=x.dtype)
    _, prev_states = lax.scan(step, init, (states.transpose(1, 0, 2, 3, 4, 5),
                                           chunk_decay.transpose(1, 0, 2, 3)))
    prev_states = prev_states.transpose(1, 0, 2, 3, 4, 5)
    y_off = jnp.einsum('bclgn,bcgjpn,bcgjl->bclgjp', Cm, prev_states, jnp.exp(a_cs))
    return (y_diag + y_off).reshape(b, S, H, P)


def ssd_branch(z, xBC, dt_raw, conv_w, conv_b, dt_bias, a_log, d_skip, ssd_norm):
    b, S, _ = z.shape
    xBC = jax.nn.silu(causal_depthwise_conv(xBC, conv_w, conv_b))
    xs, Bm, Cm = jnp.split(xBC, [SSD_INNER, SSD_INNER + SSD_GROUPS * SSD_STATE], axis=-1)
    xs = xs.reshape(b, S, SSD_HEADS, SSD_HEAD_DIM).astype(jnp.float32)
    Bm = Bm.reshape(b, S, SSD_GROUPS, SSD_STATE).astype(jnp.float32)
    Cm = Cm.reshape(b, S, SSD_GROUPS, SSD_STATE).astype(jnp.float32)
    dt = jax.nn.softplus(dt_raw.astype(jnp.float32) + dt_bias.astype(jnp.float32))
    A = -jnp.exp(a_log.astype(jnp.float32))
    y = ssd_chunked(xs * dt[..., None], dt * A, Bm, Cm)
    y = y + xs * d_skip.astype(jnp.float32)[:, None]
    y = y.reshape(b, S, SSD_INNER).astype(z.dtype) * jax.nn.silu(z)
    y = rmsnorm(y.reshape(b, S, SSD_GROUPS, SSD_INNER // SSD_GROUPS),
                ssd_norm.reshape(SSD_GROUPS, SSD_INNER // SSD_GROUPS))
    return y.reshape(b, S, SSD_INNER)


def setup_inputs(seed: int = 0) -> dict:
    key = jax.random.key(seed)
    ks = jax.random.split(key, 24)
    f32 = jnp.float32

    def normal(k, shape, scale):
        return jax.random.normal(k, shape, f32) * scale

    def gain(k, n):
        return 1.0 + 0.02 * jax.random.normal(k, (DEPTH, n), f32)

    dt = jnp.exp(jax.random.uniform(ks[12], (DEPTH, SSD_HEADS), f32,
                                    minval=math.log(1e-3), maxval=math.log(1e-1)))
    return {
        "x": normal(ks[0], (BATCH, SEQ, D_MODEL), 1.0),
        "ffn1_norm": gain(ks[1], D_MODEL),
        "ffn1_w_gate": normal(ks[2], (DEPTH, D_MODEL, D_FF), D_MODEL ** -0.5),
        "ffn1_w_up": normal(ks[3], (DEPTH, D_MODEL, D_FF), D_MODEL ** -0.5),
        "ffn1_w_down": normal(ks[4], (DEPTH, D_FF, D_MODEL), D_FF ** -0.5),
        "mix_norm": gain(ks[5], D_MODEL),
        "w_in": normal(ks[6], (DEPTH, D_MODEL, IN_COLS), D_MODEL ** -0.5),
        "q_norm": gain(ks[7], ATTN_HEAD_DIM),
        "k_norm": gain(ks[8], ATTN_HEAD_DIM),
        "conv_w": normal(ks[9], (DEPTH, SSD_CONV, SSD_CONV_DIM), SSD_CONV ** -0.5),
        "conv_b": normal(ks[10], (DEPTH, SSD_CONV_DIM), 0.02),
        "dt_bias": dt + jnp.log(-jnp.expm1(-dt)),
        "a_log": jnp.log(jax.random.uniform(ks[13], (DEPTH, SSD_HEADS), f32, minval=1.0, maxval=16.0)),
        "d_skip": 1.0 + 0.1 * jax.random.normal(ks[14], (DEPTH, SSD_HEADS), f32),
        "ssd_norm": gain(ks[15], SSD_INNER),
        "w_attn_branch": normal(ks[16], (DEPTH, ATTN_OUT, D_MODEL), ATTN_OUT ** -0.5),
        "w_ssd_branch": normal(ks[17], (DEPTH, SSD_INNER, D_MODEL), SSD_INNER ** -0.5),
        "w_out": normal(ks[18], (DEPTH, D_MODEL, D_MODEL), D_MODEL ** -0.5),
        "ffn2_norm": gain(ks[19], D_MODEL),
        "ffn2_w_gate": normal(ks[20], (DEPTH, D_MODEL, D_FF), D_MODEL ** -0.5),
        "ffn2_w_up": normal(ks[21], (DEPTH, D_MODEL, D_FF), D_MODEL ** -0.5),
        "ffn2_w_down": normal(ks[22], (DEPTH, D_FF, D_MODEL), D_FF ** -0.5),
    }


def reference(x, ffn1_norm, ffn1_w_gate, ffn1_w_up, ffn1_w_down, mix_norm, w_in,
              q_norm, k_norm, conv_w, conv_b, dt_bias, a_log, d_skip, ssd_norm,
              w_attn_branch, w_ssd_branch, w_out, ffn2_norm, ffn2_w_gate, ffn2_w_up,
              ffn2_w_down):
    split_at = np.cumsum([ATTN_QKV, ATTN_QKV, ATTN_QKV, SSD_INNER, SSD_CONV_DIM,
                          SSD_HEADS, D_MODEL]).tolist()
    for l in range(DEPTH):
        x = x + 0.5 * swiglu(rmsnorm(x, ffn1_norm[l]), ffn1_w_gate[l], ffn1_w_up[l], ffn1_w_down[l])
        h = rmsnorm(x, mix_norm[l])
        proj = h @ w_in[l]
        q, k, v, z, xBC, dt_raw, g_attn, g_ssd = jnp.split(proj, split_at, axis=-1)
        a = attention_branch(q, k, v, q_norm[l], k_norm[l]) @ w_attn_branch[l]
        s = ssd_branch(z, xBC, dt_raw, conv_w[l], conv_b[l], dt_bias[l], a_log[l],
                       d_skip[l], ssd_norm[l]) @ w_ssd_branch[l]
        merged = jax.nn.sigmoid(g_attn) * a + jax.nn.sigmoid(g_ssd) * s
        x = x + merged @ w_out[l]
        x = x + 0.5 * swiglu(rmsnorm(x, ffn2_norm[l]), ffn2_w_gate[l], ffn2_w_up[l], ffn2_w_down[l])
    return x
```

```python
import functools
import math

import jax
import jax.numpy as jnp
import numpy as np
from jax import lax
from jax.experimental import pallas as pl
from jax.experimental.pallas import tpu as pltpu

F32 = jnp.float32
BF16 = jnp.bfloat16

D_MODEL = 1024
HEAD_DIM = 64
ATTN_PATTERNS = ((128, 1), (512, 4), (2048, 16))
HEADS_PER_PATTERN = 8
N_ATTN_HEADS = HEADS_PER_PATTERN * len(ATTN_PATTERNS)
ATTN_QKV = N_ATTN_HEADS * HEAD_DIM
ATTN_OUT = HEADS_PER_PATTERN * HEAD_DIM
ATTN_BLOCK = 128
ALIBI_MAX_EXP = 8.0
SSD_INNER = 2048
SSD_HEADS = 32
SSD_STATE = 128
SSD_GROUPS = 4
SSD_CONV = 4
SSD_CHUNK = 128
SSD_GROUP_COLS = SSD_INNER // SSD_GROUPS
SSD_CONV_DIM = SSD_INNER + 2 * SSD_GROUPS * SSD_STATE
D_FF = 2816
EPS = 1e-6

LANES = 128
SUBLANES = 8
MXU_DIM = 256
VMEM_LIMIT_BYTES = 56 * 1024 * 1024

FFN_ROWS = 512
FFN_CHUNK = 256
PROJ_ROWS = 256
PROJ_CHUNK = 512
ATTN_ROWS = 256
SSD_ROWS = 512
MERGE_ROWS = 512
DT_COLS = LANES

OFF_Q = 0
OFF_K = ATTN_QKV
OFF_V = 2 * ATTN_QKV
OFF_Z = 3 * ATTN_QKV
OFF_XBC = OFF_Z + SSD_INNER
OFF_DT = OFF_XBC + SSD_CONV_DIM
OFF_GATE = OFF_DT + DT_COLS
PROJ_COLS = OFF_GATE + 2 * D_MODEL

NEG_BIG = -1e30


def _const_spec(shape):
    nd = len(shape)
    return pl.BlockSpec(shape, lambda *_: (0,) * nd, pipeline_mode=pl.Buffered(1))


def _params(*sem):
    return pltpu.CompilerParams(dimension_semantics=sem, vmem_limit_bytes=VMEM_LIMIT_BYTES)


def _rms(x, gain):
    ms = jnp.mean(x * x, axis=-1, keepdims=True)
    return x * lax.rsqrt(ms + EPS) * gain


def _silu(x):
    return x * jax.nn.sigmoid(x)


def _dot(a, b):
    return jnp.dot(a, b, preferred_element_type=F32)


def _dot_nt(a, b):
    return lax.dot_general(a, b, (((1,), (1,)), ((), ())), preferred_element_type=F32)


def _split2(x):
    hi = x.astype(BF16)
    lo = (x - hi.astype(F32)).astype(BF16)
    return hi, lo


def _split3(x):
    a = x.astype(BF16)
    r = x - a.astype(F32)
    b = r.astype(BF16)
    c = (r - b.astype(F32)).astype(BF16)
    return a, b, c


def _ffn_body(x_ref, gain_ref, wg_ref, wu_ref, wd_ref, o_ref, act_ref):
    x = x_ref[...]
    h = _rms(x, gain_ref[...]).astype(BF16)
    for c in range(D_FF // FFN_CHUNK):
        cs = pl.ds(c * FFN_CHUNK, FFN_CHUNK)
        g = _dot(h, wg_ref[:, cs])
        u = _dot(h, wu_ref[:, cs])
        act_ref[:, cs] = (_silu(g) * u).astype(BF16)
    y = _dot(act_ref[...], wd_ref[...])
    o_ref[...] = x + 0.5 * y


def _ffn(x2d, gain, wg, wu, wd):
    t = x2d.shape[0]
    assert t % FFN_ROWS == 0
    row = pl.BlockSpec((FFN_ROWS, D_MODEL), lambda i: (i, 0))
    return pl.pallas_call(
        _ffn_body,
        grid=(t // FFN_ROWS,),
        in_specs=[row, _const_spec((1, D_MODEL)), _const_spec((D_MODEL, D_FF)),
                  _const_spec((D_MODEL, D_FF)), _const_spec((D_FF, D_MODEL))],
        out_specs=row,
        out_shape=jax.ShapeDtypeStruct((t, D_MODEL), F32),
        scratch_shapes=[pltpu.VMEM((FFN_ROWS, D_FF), BF16)],
        compiler_params=_params("parallel"),
        name="ffn",
    )(x2d, gain, wg, wu, wd)


def _proj_body(x_ref, gain_ref, w_ref, qgain_ref, kgain_ref, headmean_ref,
               qkv_ref, z_ref, xbc_ref, dt_ref, gate_ref):
    h = _rms(x_ref[...], gain_ref[...]).astype(BF16)

    def chunk(off, c):
        return _dot(h, w_ref[:, pl.ds(off + c * PROJ_CHUNK, PROJ_CHUNK)])

    for c in range(2 * ATTN_QKV // PROJ_CHUNK):
        y = chunk(OFF_Q, c)
        yy = (y * y).astype(BF16)
        ms = jnp.concatenate(
            [_dot(yy[:, j * MXU_DIM:(j + 1) * MXU_DIM], headmean_ref[...])
             for j in range(PROJ_CHUNK // MXU_DIM)], axis=1)
        gain = qgain_ref[...] if c < ATTN_QKV // PROJ_CHUNK else kgain_ref[...]
        qkv_ref[:, pl.ds(c * PROJ_CHUNK, PROJ_CHUNK)] = (y * lax.rsqrt(ms + EPS) * gain).astype(BF16)
    for c in range(ATTN_QKV // PROJ_CHUNK):
        qkv_ref[:, pl.ds(OFF_V + c * PROJ_CHUNK, PROJ_CHUNK)] = chunk(OFF_V, c).astype(BF16)
    for c in range(SSD_INNER // PROJ_CHUNK):
        z_ref[:, pl.ds(c * PROJ_CHUNK, PROJ_CHUNK)] = chunk(OFF_Z, c).astype(BF16)
    for c in range(SSD_CONV_DIM // PROJ_CHUNK):
        xbc_ref[:, pl.ds(c * PROJ_CHUNK, PROJ_CHUNK)] = chunk(OFF_XBC, c).astype(BF16)
    dt_ref[...] = _dot(h, w_ref[:, pl.ds(OFF_DT, DT_COLS)])
    for c in range(2 * D_MODEL // PROJ_CHUNK):
        gate_ref[:, pl.ds(c * PROJ_CHUNK, PROJ_CHUNK)] = chunk(OFF_GATE, c).astype(BF16)


def _proj(x2d, gain, w_packed, qgain, kgain, headmean):
    t = x2d.shape[0]
    assert t % PROJ_ROWS == 0

    def row(cols):
        return pl.BlockSpec((PROJ_ROWS, cols), lambda i: (i, 0))

    return pl.pallas_call(
        _proj_body,
        grid=(t // PROJ_ROWS,),
        in_specs=[row(D_MODEL), _const_spec((1, D_MODEL)), _const_spec((D_MODEL, PROJ_COLS)),
                  _const_spec((1, PROJ_CHUNK)), _const_spec((1, PROJ_CHUNK)),
                  _const_spec((MXU_DIM, MXU_DIM))],
        out_specs=[row(3 * ATTN_QKV), row(SSD_INNER), row(SSD_CONV_DIM), row(DT_COLS), row(2 * D_MODEL)],
        out_shape=[jax.ShapeDtypeStruct((t, 3 * ATTN_QKV), BF16),
                   jax.ShapeDtypeStruct((t, SSD_INNER), BF16),
                   jax.ShapeDtypeStruct((t, SSD_CONV_DIM), BF16),
                   jax.ShapeDtypeStruct((t, DT_COLS), F32),
                   jax.ShapeDtypeStruct((t, 2 * D_MODEL), BF16)],
        compiler_params=_params("parallel"),
        name="proj",
    )(x2d, gain, w_packed, qgain, kgain, headmean)


def _attn_body(q_ref, kc_ref, kp_ref, vc_ref, vp_ref, bias_c_ref, bias_p_ref, o_ref, lse_ref):
    first_penalty = jnp.where(pl.program_id(2) == 0, NEG_BIG, 0.0)
    lane = lax.broadcasted_iota(jnp.int32, (ATTN_BLOCK, LANES), 1)
    even = lane < HEAD_DIM
    zero = jnp.zeros((ATTN_BLOCK, LANES), BF16)
    for i in range(ATTN_ROWS // ATTN_BLOCK):
        rows = pl.ds(i * ATTN_BLOCK, ATTN_BLOCK)
        lse_tile = jnp.zeros((ATTN_BLOCK, LANES), F32)
        for p in range(HEADS_PER_PATTERN // 2):
            cols = pl.ds(p * LANES, LANES)
            q2 = q_ref[rows, cols]
            k_cur, v_cur = kc_ref[rows, cols], vc_ref[rows, cols]
            if i == 0:
                k_prev, v_prev = kp_ref[:, cols], vp_ref[:, cols]
            else:
                prev_rows = pl.ds((i - 1) * ATTN_BLOCK, ATTN_BLOCK)
                k_prev, v_prev = kc_ref[prev_rows, cols], vc_ref[prev_rows, cols]
            qq = jnp.concatenate([jnp.where(even, q2, zero), jnp.where(even, zero, q2)], axis=0)
            s_cur = _dot_nt(qq, k_cur)
            s_prev = _dot_nt(qq, k_prev)
            probs, inv_l = [], []
            for hh in range(2):
                h = 2 * p + hh
                hrows = slice(hh * ATTN_BLOCK, (hh + 1) * ATTN_BLOCK)
                sc = s_cur[hrows] + bias_c_ref[h]
                sp = s_prev[hrows] + bias_p_ref[h]
                if i == 0:
                    sp = sp + first_penalty
                m = jnp.maximum(jnp.max(sc, axis=-1, keepdims=True), jnp.max(sp, axis=-1, keepdims=True))
                pc = jnp.exp(sc - m)
                pp = jnp.exp(sp - m)
                l = jnp.sum(pc, axis=-1, keepdims=True) + jnp.sum(pp, axis=-1, keepdims=True)
                probs += [pp.astype(BF16), pc.astype(BF16)]
                inv_l.append(1.0 / l)
                lse_tile = jnp.where(lane == h, m + jnp.log(l), lse_tile)
            pmat = jnp.concatenate(probs, axis=1)
            vmat = jnp.concatenate([jnp.where(even, v_prev, zero), jnp.where(even, v_cur, zero),
                                    jnp.where(even, zero, v_prev), jnp.where(even, zero, v_cur)], axis=0)
            o_pair = _dot(pmat, vmat)
            o_ref[rows, cols] = o_pair * jnp.where(even, inv_l[0], inv_l[1])
        lse_ref[rows, :] = lse_tile


def _attn_pattern(qkv, bias_c, bias_p, group, dilation, batch, seq):
    stream_len = seq // dilation
    assert stream_len % ATTN_ROWS == 0
    col_blocks = 3 * ATTN_QKV // ATTN_OUT
    qkv_v = qkv.reshape(batch, stream_len, dilation * 3 * ATTN_QKV)
    sub = ATTN_ROWS // ATTN_BLOCK
    groups = len(ATTN_PATTERNS)

    def cur(which):
        return pl.BlockSpec((None, ATTN_ROWS, ATTN_OUT),
                            lambda b, r, n: (b, n, r * col_blocks + which * groups + group))

    def prev(which):
        return pl.BlockSpec((None, ATTN_BLOCK, ATTN_OUT),
                            lambda b, r, n: (b, jnp.maximum(n * sub - 1, 0),
                                             r * col_blocks + which * groups + group))

    o, lse = pl.pallas_call(
        _attn_body,
        grid=(batch, dilation, stream_len // ATTN_ROWS),
        in_specs=[cur(0), cur(1), prev(1), cur(2), prev(2),
                  _const_spec(bias_c.shape), _const_spec(bias_p.shape)],
        out_specs=[pl.BlockSpec((None, ATTN_ROWS, ATTN_OUT), lambda b, r, n: (b, n, r)),
                   pl.BlockSpec((None, ATTN_ROWS, LANES), lambda b, r, n: (b, n, r))],
        out_shape=[jax.ShapeDtypeStruct((batch, stream_len, dilation * ATTN_OUT), F32),
                   jax.ShapeDtypeStruct((batch, stream_len, dilation * LANES), F32)],
        compiler_params=_params("parallel", "parallel", "parallel"),
        name=f"attn_d{dilation}",
    )(qkv_v, qkv_v, qkv_v, qkv_v, qkv_v, bias_c, bias_p)
    return o.reshape(batch * seq, ATTN_OUT), lse.reshape(batch * seq, LANES)


def _attn_bias(group, dilation):
    h = np.arange(group * HEADS_PER_PATTERN, (group + 1) * HEADS_PER_PATTERN, dtype=np.float64)
    slopes = np.exp2(-ALIBI_MAX_EXP * (h + 1) / N_ATTN_HEADS).astype(np.float32)
    a = np.arange(ATTN_BLOCK)[:, None]
    c = np.arange(ATTN_BLOCK)[None, :]
    rel_cur = (a - c).astype(np.float32)
    rel_prev = rel_cur + ATTN_BLOCK
    coef = -slopes[:, None, None]
    bias_c = np.where((a >= c)[None], coef * (rel_cur * dilation)[None], NEG_BIG).astype(np.float32)
    bias_p = np.where((c >= a)[None], coef * (rel_prev * dilation)[None], NEG_BIG).astype(np.float32)
    return jnp.asarray(bias_c), jnp.asarray(bias_p)


def _ssd_body(z_ref, xbc_ref, dt_ref, convw_ref, convb_ref, dtbias_ref, alog_ref, dskip_ref,
              norm_ref, expand_ref, tril_ref, w_ref, o_ref,
              tail_ref, ext_ref, xs_ref, b_ref, c_ref, yn_ref, state_ref):
    @pl.when(pl.program_id(1) == 0)
    def _():
        tail_ref[...] = jnp.zeros_like(tail_ref)
        state_ref[...] = jnp.zeros_like(state_ref)

    for c in range(SSD_CONV_DIM // SSD_GROUP_COLS):
        cs = pl.ds(c * SSD_GROUP_COLS, SSD_GROUP_COLS)
        cur = xbc_ref[:, cs].astype(F32)
        ext_ref[pl.ds(0, SUBLANES), :] = tail_ref[:, cs]
        ext_ref[pl.ds(SUBLANES, SSD_ROWS), :] = cur
        tail_ref[:, cs] = cur[SSD_ROWS - SUBLANES:, :]
        acc = convb_ref[:, cs] + convw_ref[pl.ds(SSD_CONV - 1, 1), cs] * cur
        for back in range(1, SSD_CONV):
            acc = acc + convw_ref[pl.ds(SSD_CONV - 1 - back, 1), cs] * ext_ref[pl.ds(SUBLANES - back, SSD_ROWS), :]
        act = _silu(acc)
        if c < SSD_GROUPS:
            xs_ref[:, cs] = act
        elif c == SSD_GROUPS:
            b_ref[...] = act
        else:
            c_ref[...] = act

    lane = lax.broadcasted_iota(jnp.int32, (SSD_CHUNK, LANES), 1)
    even = lane < HEAD_DIM
    row_i = lax.broadcasted_iota(jnp.int32, (SSD_CHUNK, SSD_CHUNK), 0)
    col_i = lax.broadcasted_iota(jnp.int32, (SSD_CHUNK, SSD_CHUNK), 1)
    causal = row_i >= col_i
    zero_b = jnp.zeros((SSD_CHUNK, LANES), BF16)

    def expand(p, g):
        hi, lo = _split2(p)
        e = expand_ref[:, pl.ds(g * SSD_GROUP_COLS, SSD_GROUP_COLS)]
        return _dot(hi, e) + _dot(lo, e)

    def chunk_step(ci, carry):
        rows = pl.ds(pl.multiple_of(ci * SSD_CHUNK, SSD_CHUNK), SSD_CHUNK)
        dt_pre = dt_ref[rows, :] + dtbias_ref[...]
        dt = jnp.maximum(dt_pre, 0.0) + jnp.log(1.0 + jnp.exp(-jnp.abs(dt_pre)))
        a = dt * -jnp.exp(alog_ref[...])
        a_cs = sum(_dot(tril_ref[...], t) for t in _split3(a))
        a_cs_t = a_cs.T
        a_last = a_cs[SSD_CHUNK - 1:SSD_CHUNK, :]
        p_out = jnp.exp(a_cs)
        p_state = jnp.exp(a_last - a_cs) * dt
        for g in range(SSD_GROUPS):
            gcols = pl.ds(g * SSD_GROUP_COLS, SSD_GROUP_COLS)
            ncols = pl.ds(g * SSD_STATE, SSD_STATE)
            x = xs_ref[rows, gcols]
            e_out = expand(p_out, g)
            x_dt = (x * expand(dt, g)).astype(BF16)
            x_state = (x * expand(p_state, g)).astype(BF16)
            bm = b_ref[rows, ncols]
            cm = c_ref[rows, ncols].astype(BF16)
            cb = _dot_nt(cm, bm.astype(BF16))
            state = state_ref[g]
            y = _dot(cm, state.astype(BF16)) * e_out
            state_ref[g] = state * e_out[SSD_CHUNK - 1:SSD_CHUNK, :] + _dot(bm.T.astype(BF16), x_state)
            pairs = []
            for p in range(SSD_GROUP_COLS // LANES):
                pcols = slice(p * LANES, (p + 1) * LANES)
                lmats = []
                for hh in range(2):
                    h = g * (SSD_HEADS // SSD_GROUPS) + 2 * p + hh
                    diff = a_cs[:, h:h + 1] - a_cs_t[h:h + 1, :]
                    lmats.append((cb * jnp.exp(jnp.where(causal, diff, NEG_BIG))).astype(BF16))
                xp = x_dt[:, pcols]
                rhs = jnp.concatenate([jnp.where(even, xp, zero_b), jnp.where(even, zero_b, xp)], axis=0)
                pairs.append(_dot(jnp.concatenate(lmats, axis=1), rhs))
            y = y + jnp.concatenate(pairs, axis=1) + x * dskip_ref[:, gcols]
            y = y * _silu(z_ref[rows, gcols].astype(F32))
            yn_ref[rows, gcols] = _rms(y, norm_ref[:, gcols]).astype(BF16)
        return carry

    lax.fori_loop(0, SSD_ROWS // SSD_CHUNK, chunk_step, 0)
    o_ref[...] = _dot(yn_ref[...], w_ref[...])


def _ssd(z, xbc, dt, convw, convb, dtbias, alog, dskip, norm, expand, tril, w, batch, seq):
    assert seq % SSD_ROWS == 0
    tiles = seq // SSD_ROWS

    def row(cols):
        return pl.BlockSpec((SSD_ROWS, cols), lambda b, s: (b * tiles + s, 0))

    consts = [convw, convb, dtbias, alog, dskip, norm, expand, tril, w]
    return pl.pallas_call(
        _ssd_body,
        grid=(batch, tiles),
        in_specs=[row(SSD_INNER), row(SSD_CONV_DIM), row(DT_COLS)] + [_const_spec(c.shape) for c in consts],
        out_specs=row(D_MODEL),
        out_shape=jax.ShapeDtypeStruct((batch * seq, D_MODEL), F32),
        scratch_shapes=[
            pltpu.VMEM((SUBLANES, SSD_CONV_DIM), F32),
            pltpu.VMEM((SUBLANES + SSD_ROWS, SSD_GROUP_COLS), F32),
            pltpu.VMEM((SSD_ROWS, SSD_INNER), F32),
            pltpu.VMEM((SSD_ROWS, SSD_GROUPS * SSD_STATE), F32),
            pltpu.VMEM((SSD_ROWS, SSD_GROUPS * SSD_STATE), F32),
            pltpu.VMEM((SSD_ROWS, SSD_INNER), BF16),
            pltpu.VMEM((SSD_GROUPS, SSD_STATE, SSD_GROUP_COLS), F32),
        ],
        compiler_params=_params("parallel", "arbitrary"),
        name="ssd",
    )(z, xbc, dt, *consts)


def _merge_body(o0_ref, o1_ref, o2_ref, l0_ref, l1_ref, l2_ref, s_ref, gate_ref, x_ref,
                expand_ref, wa_ref, wo_ref, out_ref):
    lses = [l0_ref[...], l1_ref[...], l2_ref[...]]
    m = jnp.maximum(jnp.maximum(lses[0], lses[1]), lses[2])
    es = [jnp.exp(l - m) for l in lses]
    inv = 1.0 / (es[0] + es[1] + es[2])
    attn = None
    for e, o_ref in zip(es, (o0_ref, o1_ref, o2_ref)):
        hi, lo = _split2(e * inv)
        w = _dot(hi, expand_ref[...]) + _dot(lo, expand_ref[...])
        attn = w * o_ref[...] if attn is None else attn + w * o_ref[...]
    a = _dot(attn.astype(BF16), wa_ref[...])
    g_attn = gate_ref[:, pl.ds(0, D_MODEL)].astype(F32)
    g_ssd = gate_ref[:, pl.ds(D_MODEL, D_MODEL)].astype(F32)
    merged = jax.nn.sigmoid(g_attn) * a + jax.nn.sigmoid(g_ssd) * s_ref[...]
    out_ref[...] = x_ref[...] + _dot(merged.astype(BF16), wo_ref[...])


def _merge(os_, lses, s, gates, x1, expand, wa, wo):
    t = x1.shape[0]
    assert t % MERGE_ROWS == 0

    def row(cols):
        return pl.BlockSpec((MERGE_ROWS, cols), lambda i: (i, 0))

    return pl.pallas_call(
        _merge_body,
        grid=(t // MERGE_ROWS,),
        in_specs=[row(ATTN_OUT)] * 3 + [row(LANES)] * 3 + [row(D_MODEL), row(2 * D_MODEL), row(D_MODEL),
                  _const_spec(expand.shape), _const_spec(wa.shape), _const_spec(wo.shape)],
        out_specs=row(D_MODEL),
        out_shape=jax.ShapeDtypeStruct((t, D_MODEL), F32),
        compiler_params=_params("parallel"),
        name="merge",
    )(*os_, *lses, s, gates, x1, expand, wa, wo)


def _head_expand(n_heads, rows):
    e = np.zeros((rows, n_heads * HEAD_DIM), np.float32)
    for h in range(n_heads):
        e[h, h * HEAD_DIM:(h + 1) * HEAD_DIM] = 1.0
    return jnp.asarray(e, BF16)


def _pad_lanes(v, width):
    return jnp.pad(v.astype(F32), (0, width - v.shape[0]))[None, :]


def kernel(x, ffn1_norm, ffn1_w_gate, ffn1_w_up, ffn1_w_down, mix_norm, w_in, q_norm, k_norm, conv_w, conv_b, dt_bias, a_log, d_skip, ssd_norm, w_attn_branch, w_ssd_branch, w_out, ffn2_norm, ffn2_w_gate, ffn2_w_up, ffn2_w_down):
    batch, seq, _ = x.shape
    depth = ffn1_norm.shape[0]
    x2d = x.reshape(batch * seq, D_MODEL)

    headmean = np.zeros((MXU_DIM, MXU_DIM), np.float32)
    for h in range(MXU_DIM // HEAD_DIM):
        headmean[h * HEAD_DIM:(h + 1) * HEAD_DIM, h * HEAD_DIM:(h + 1) * HEAD_DIM] = 1.0 / HEAD_DIM
    headmean = jnp.asarray(headmean, BF16)
    tril = jnp.asarray(np.tril(np.ones((SSD_CHUNK, SSD_CHUNK), np.float32)), BF16)
    expand_ssd = _head_expand(SSD_HEADS, LANES)
    expand_attn = _head_expand(HEADS_PER_PATTERN, LANES)
    biases = [_attn_bias(g, d) for g, (_, d) in enumerate(ATTN_PATTERNS)]

    for l in range(depth):
        x2d = _ffn(x2d, ffn1_norm[l][None, :], ffn1_w_gate[l].astype(BF16), ffn1_w_up[l].astype(BF16),
                   ffn1_w_down[l].astype(BF16))

        w = w_in[l]
        n_dt = SSD_HEADS
        w_packed = jnp.concatenate(
            [w[:, :OFF_DT], jnp.pad(w[:, OFF_DT:OFF_DT + n_dt], ((0, 0), (0, DT_COLS - n_dt))),
             w[:, OFF_DT + n_dt:]], axis=1).astype(BF16)
        reps = PROJ_CHUNK // HEAD_DIM
        qgain = jnp.tile(q_norm[l].astype(F32) * (1.0 / math.sqrt(HEAD_DIM)), reps)[None, :]
        kgain = jnp.tile(k_norm[l].astype(F32), reps)[None, :]
        qkv, z, xbc, dt, gates = _proj(x2d, mix_norm[l][None, :], w_packed, qgain, kgain, headmean)

        os_, lses = [], []
        for g, (_, dilation) in enumerate(ATTN_PATTERNS):
            o, lse = _attn_pattern(qkv, biases[g][0], biases[g][1], g, dilation, batch, seq)
            os_.append(o)
            lses.append(lse)

        s = _ssd(z, xbc, dt, conv_w[l].astype(F32), conv_b[l].astype(F32)[None, :],
                 _pad_lanes(dt_bias[l], DT_COLS), _pad_lanes(a_log[l], DT_COLS),
                 jnp.repeat(d_skip[l].astype(F32), HEAD_DIM)[None, :], ssd_norm[l].astype(F32)[None, :],
                 expand_ssd, tril, w_ssd_branch[l].astype(BF16), batch, seq)

        x2d = _merge(os_, lses, s, gates, x2d, expand_attn, w_attn_branch[l].astype(BF16),
                     w_out[l].astype(BF16))

        x2d = _ffn(x2d, ffn2_norm[l][None, :], ffn2_w_gate[l].astype(BF16), ffn2_w_up[l].astype(BF16),
                   ffn2_w_down[l].astype(BF16))
    return x2d.reshape(batch, seq, D_MODEL)
```

```python
import math

import jax
import jax.numpy as jnp
import numpy as np
from jax import lax
from jax.experimental import pallas as pl
from jax.experimental.pallas import tpu as pltpu

F32 = jnp.float32
BF16 = jnp.bfloat16

D_MODEL = 1024
HEAD_DIM = 64
ATTN_PATTERNS = ((128, 1), (512, 4), (2048, 16))
HEADS_PER_PATTERN = 8
N_ATTN_HEADS = HEADS_PER_PATTERN * len(ATTN_PATTERNS)
ATTN_QKV = N_ATTN_HEADS * HEAD_DIM
ATTN_OUT = HEADS_PER_PATTERN * HEAD_DIM
ATTN_BLOCK = 128
ALIBI_MAX_EXP = 8.0
SSD_INNER = 2048
SSD_HEADS = 32
SSD_STATE = 128
SSD_GROUPS = 4
SSD_CONV = 4
SSD_CHUNK = 128
SSD_GROUP_COLS = SSD_INNER // SSD_GROUPS
SSD_GROUP_HEADS = SSD_HEADS // SSD_GROUPS
SSD_CONV_DIM = SSD_INNER + 2 * SSD_GROUPS * SSD_STATE
D_FF = 2816
EPS = 1e-6

LANES = 128
SUBLANES = 8
MXU_DIM = 256
VMEM_LIMIT_BYTES = 56 * 1024 * 1024

FFN_ROWS = 512
FFN_CHUNK = 256
PROJ_ROWS = 256
PROJ_CHUNK = 512
ATTN_ROWS = 256
SSD_ROWS = 512
MERGE_ROWS = 512
DT_COLS = LANES

PATTERN_COLS = 3 * ATTN_OUT
OFF_Z = len(ATTN_PATTERNS) * PATTERN_COLS
OFF_XBC = OFF_Z + SSD_INNER
OFF_DT = OFF_XBC + SSD_CONV_DIM
OFF_GATE = OFF_DT + DT_COLS
PROJ_COLS = OFF_GATE + 2 * D_MODEL

NEG_BIG = -1e30


def _const_spec(shape):
    nd = len(shape)
    return pl.BlockSpec(shape, lambda *_: (0,) * nd, pipeline_mode=pl.Buffered(1))


def _params(*sem):
    return pltpu.CompilerParams(dimension_semantics=sem, vmem_limit_bytes=VMEM_LIMIT_BYTES)


def _rms(x, gain):
    ms = jnp.mean(x * x, axis=-1, keepdims=True)
    return x * lax.rsqrt(ms + EPS) * gain


def _silu(x):
    return x * jax.nn.sigmoid(x)


def _dot(a, b):
    return jnp.dot(a, b, preferred_element_type=F32)


def _dot_nt(a, b):
    return lax.dot_general(a, b, (((1,), (1,)), ((), ())), preferred_element_type=F32)


def _split2(x):
    hi = x.astype(BF16)
    lo = (x - hi.astype(F32)).astype(BF16)
    return jnp.concatenate([hi, lo], axis=1)


def _split3(x):
    a = x.astype(BF16)
    r = x - a.astype(F32)
    b = r.astype(BF16)
    c = (r - b.astype(F32)).astype(BF16)
    return a, b, c


def _ffn_body(x_ref, gain_ref, wg_ref, wu_ref, wd_ref, o_ref, act_ref):
    x = x_ref[...]
    h = _rms(x, gain_ref[...]).astype(BF16)
    for c in range(D_FF // FFN_CHUNK):
        cs = pl.ds(c * FFN_CHUNK, FFN_CHUNK)
        g = _dot(h, wg_ref[:, cs])
        u = _dot(h, wu_ref[:, cs])
        act_ref[:, cs] = (_silu(g) * u).astype(BF16)
    y = _dot(act_ref[...], wd_ref[...])
    o_ref[...] = x + 0.5 * y


def _ffn(x2d, gain, wg, wu, wd):
    t = x2d.shape[0]
    assert t % FFN_ROWS == 0
    row = pl.BlockSpec((FFN_ROWS, D_MODEL), lambda i: (i, 0))
    return pl.pallas_call(
        _ffn_body,
        grid=(t // FFN_ROWS,),
        in_specs=[row, _const_spec((1, D_MODEL)), _const_spec((D_MODEL, D_FF)),
                  _const_spec((D_MODEL, D_FF)), _const_spec((D_FF, D_MODEL))],
        out_specs=row,
        out_shape=jax.ShapeDtypeStruct((t, D_MODEL), F32),
        scratch_shapes=[pltpu.VMEM((FFN_ROWS, D_FF), BF16)],
        compiler_params=_params("parallel"),
        name="ffn",
    )(x2d, gain, wg, wu, wd)


def _proj_body(x_ref, gain_ref, w_ref, qgain_ref, kgain_ref, headmean_ref, convw_ref, convb_ref,
               qkv0_ref, qkv1_ref, qkv2_ref, z_ref, xbc_ref, dt_ref, gate_ref, ext_ref, slab_ref):
    gain = gain_ref[...]

    def normed(rows):
        return _rms(rows, gain).astype(BF16)

    def chunk(h, off, c):
        return _dot(h, w_ref[:, pl.ds(off + c * PROJ_CHUNK, PROJ_CHUNK)])

    h = normed(x_ref[...])
    for j in range(D_MODEL // LANES):
        slab_ref[j] = x_ref[:, pl.ds(j * LANES, LANES)]

    for g, ((_, dilation), out_ref) in enumerate(zip(ATTN_PATTERNS, (qkv0_ref, qkv1_ref, qkv2_ref))):
        per_stream = PROJ_ROWS // dilation
        if dilation == 1:
            hg = h
        else:
            hg = normed(jnp.concatenate(
                [jnp.concatenate([slab_ref[j, pl.ds(r, per_stream, stride=dilation), :]
                                  for j in range(D_MODEL // LANES)], axis=1)
                 for r in range(dilation)], axis=0))
        for part in range(3):
            y = chunk(hg, g * PATTERN_COLS, part)
            if part < 2:
                yy = (y * y).astype(BF16)
                ms = jnp.concatenate(
                    [_dot(yy[:, j * MXU_DIM:(j + 1) * MXU_DIM], headmean_ref[...])
                     for j in range(PROJ_CHUNK // MXU_DIM)], axis=1)
                y = y * lax.rsqrt(ms + EPS) * (qgain_ref[...] if part == 0 else kgain_ref[...])
            y = y.astype(BF16)
            cols = pl.ds(part * PROJ_CHUNK, PROJ_CHUNK)
            if dilation == 1:
                out_ref[:, cols] = y
            else:
                for r in range(dilation):
                    out_ref[r, :, cols] = y[r * per_stream:(r + 1) * per_stream]

    for c in range(SSD_INNER // PROJ_CHUNK):
        z_ref[:, pl.ds(c * PROJ_CHUNK, PROJ_CHUNK)] = _silu(chunk(h, OFF_Z, c)).astype(BF16)

    @pl.when(pl.program_id(1) == 0)
    def _():
        ext_ref[pl.ds(0, SUBLANES), :] = jnp.zeros((SUBLANES, SSD_CONV_DIM), F32)

    for c in range(SSD_CONV_DIM // PROJ_CHUNK):
        cs = pl.ds(c * PROJ_CHUNK, PROJ_CHUNK)
        y = chunk(h, OFF_XBC, c)
        ext_ref[pl.ds(SUBLANES, PROJ_ROWS), cs] = y
        acc = convb_ref[:, cs] + convw_ref[pl.ds(SSD_CONV - 1, 1), cs] * y
        for back in range(1, SSD_CONV):
            acc = acc + convw_ref[pl.ds(SSD_CONV - 1 - back, 1), cs] * ext_ref[pl.ds(SUBLANES - back, PROJ_ROWS), cs]
        xbc_ref[:, cs] = _silu(acc).astype(BF16)
        ext_ref[pl.ds(0, SUBLANES), cs] = y[PROJ_ROWS - SUBLANES:, :]

    dt_ref[...] = _dot(h, w_ref[:, pl.ds(OFF_DT, DT_COLS)])
    for c in range(2 * D_MODEL // PROJ_CHUNK):
        gate_ref[:, pl.ds(c * PROJ_CHUNK, PROJ_CHUNK)] = chunk(h, OFF_GATE, c).astype(BF16)


def _proj(x2d, gain, w_packed, qgain, kgain, headmean, convw, convb, batch, seq):
    assert seq % PROJ_ROWS == 0
    tiles = seq // PROJ_ROWS

    def row(cols):
        return pl.BlockSpec((PROJ_ROWS, cols), lambda b, i: (b * tiles + i, 0))

    def streams(dilation):
        return pl.BlockSpec((None, dilation, PROJ_ROWS // dilation, PATTERN_COLS), lambda b, i: (b, 0, i, 0))

    def stream_shape(dilation):
        return jax.ShapeDtypeStruct((batch, dilation, seq // dilation, PATTERN_COLS), BF16)

    d1, d2 = ATTN_PATTERNS[1][1], ATTN_PATTERNS[2][1]
    t = batch * seq
    return pl.pallas_call(
        _proj_body,
        grid=(batch, tiles),
        in_specs=[row(D_MODEL), _const_spec((1, D_MODEL)), _const_spec((D_MODEL, PROJ_COLS)),
                  _const_spec((1, PROJ_CHUNK)), _const_spec((1, PROJ_CHUNK)),
                  _const_spec((MXU_DIM, MXU_DIM)), _const_spec(convw.shape), _const_spec(convb.shape)],
        out_specs=[row(PATTERN_COLS), streams(d1), streams(d2),
                   row(SSD_INNER), row(SSD_CONV_DIM), row(DT_COLS), row(2 * D_MODEL)],
        out_shape=[jax.ShapeDtypeStruct((t, PATTERN_COLS), BF16), stream_shape(d1), stream_shape(d2),
                   jax.ShapeDtypeStruct((t, SSD_INNER), BF16),
                   jax.ShapeDtypeStruct((t, SSD_CONV_DIM), BF16),
                   jax.ShapeDtypeStruct((t, DT_COLS), F32),
                   jax.ShapeDtypeStruct((t, 2 * D_MODEL), BF16)],
        scratch_shapes=[pltpu.VMEM((SUBLANES + PROJ_ROWS, SSD_CONV_DIM), F32),
                        pltpu.VMEM((D_MODEL // LANES, PROJ_ROWS, LANES), F32)],
        compiler_params=_params("parallel", "arbitrary"),
        name="proj",
    )(x2d, gain, w_packed, qgain, kgain, headmean, convw, convb)


def _attn_body(q_ref, kc_ref, kp_ref, vc_ref, vp_ref, bias_c_ref, bias_p_ref, o_ref, lse_ref):
    first_penalty = jnp.where(pl.program_id(2) == 0, NEG_BIG, 0.0)
    lane = lax.broadcasted_iota(jnp.int32, (ATTN_BLOCK, LANES), 1)
    even = lane < HEAD_DIM
    zero = jnp.zeros((ATTN_BLOCK, LANES), BF16)
    for i in range(ATTN_ROWS // ATTN_BLOCK):
        rows = pl.ds(i * ATTN_BLOCK, ATTN_BLOCK)
        lse_tile = jnp.zeros((ATTN_BLOCK, LANES), F32)
        for p in range(HEADS_PER_PATTERN // 2):
            cols = pl.ds(p * LANES, LANES)
            q2 = q_ref[rows, cols]
            k_cur, v_cur = kc_ref[rows, cols], vc_ref[rows, cols]
            if i == 0:
                k_prev, v_prev = kp_ref[:, cols], vp_ref[:, cols]
            else:
                prev_rows = pl.ds((i - 1) * ATTN_BLOCK, ATTN_BLOCK)
                k_prev, v_prev = kc_ref[prev_rows, cols], vc_ref[prev_rows, cols]
            qq = jnp.concatenate([jnp.where(even, q2, zero), jnp.where(even, zero, q2)], axis=0)
            s_cur = _dot_nt(qq, k_cur)
            s_prev = _dot_nt(qq, k_prev)
            probs, inv_l = [], []
            for hh in range(2):
                h = 2 * p + hh
                hrows = slice(hh * ATTN_BLOCK, (hh + 1) * ATTN_BLOCK)
                sc = s_cur[hrows] + bias_c_ref[h]
                sp = s_prev[hrows] + bias_p_ref[h]
                if i == 0:
                    sp = sp + first_penalty
                m = jnp.maximum(jnp.max(sc, axis=-1, keepdims=True), jnp.max(sp, axis=-1, keepdims=True))
                pc = jnp.exp(sc - m)
                pp = jnp.exp(sp - m)
                l = jnp.sum(pc, axis=-1, keepdims=True) + jnp.sum(pp, axis=-1, keepdims=True)
                probs += [pp.astype(BF16), pc.astype(BF16)]
                inv_l.append(1.0 / l)
                lse_tile = jnp.where(lane == h, m + jnp.log(l), lse_tile)
            pmat = jnp.concatenate(probs, axis=1)
            vmat = jnp.concatenate([jnp.where(even, v_prev, zero), jnp.where(even, v_cur, zero),
                                    jnp.where(even, zero, v_prev), jnp.where(even, zero, v_cur)], axis=0)
            o_pair = _dot(pmat, vmat)
            o_ref[rows, cols] = o_pair * jnp.where(even, inv_l[0], inv_l[1])
        lse_ref[rows, :] = lse_tile


def _attn_pattern(qkv, bias_c, bias_p):
    batch, dilation, stream_len, _ = qkv.shape
    assert stream_len % ATTN_ROWS == 0
    sub = ATTN_ROWS // ATTN_BLOCK

    def cur(part):
        return pl.BlockSpec((None, None, ATTN_ROWS, ATTN_OUT), lambda b, r, n: (b, r, n, part))

    def prev(part):
        return pl.BlockSpec((None, None, ATTN_BLOCK, ATTN_OUT),
                            lambda b, r, n: (b, r, jnp.maximum(n * sub - 1, 0), part))

    return pl.pallas_call(
        _attn_body,
        grid=(batch, dilation, stream_len // ATTN_ROWS),
        in_specs=[cur(0), cur(1), prev(1), cur(2), prev(2),
                  _const_spec(bias_c.shape), _const_spec(bias_p.shape)],
        out_specs=[pl.BlockSpec((None, None, ATTN_ROWS, ATTN_OUT), lambda b, r, n: (b, r, n, 0)),
                   pl.BlockSpec((None, None, ATTN_ROWS, LANES), lambda b, r, n: (b, r, n, 0))],
        out_shape=[jax.ShapeDtypeStruct((batch, dilation, stream_len, ATTN_OUT), F32),
                   jax.ShapeDtypeStruct((batch, dilation, stream_len, LANES), F32)],
        compiler_params=_params("parallel", "parallel", "parallel"),
        name=f"attn_d{dilation}",
    )(qkv, qkv, qkv, qkv, qkv, bias_c, bias_p)


def _attn_bias(group, dilation):
    h = np.arange(group * HEADS_PER_PATTERN, (group + 1) * HEADS_PER_PATTERN, dtype=np.float64)
    slopes = np.exp2(-ALIBI_MAX_EXP * (h + 1) / N_ATTN_HEADS).astype(np.float32)
    a = np.arange(ATTN_BLOCK)[:, None]
    c = np.arange(ATTN_BLOCK)[None, :]
    rel_cur = (a - c).astype(np.float32)
    rel_prev = rel_cur + ATTN_BLOCK
    coef = -slopes[:, None, None]
    bias_c = np.where((a >= c)[None], coef * (rel_cur * dilation)[None], NEG_BIG).astype(np.float32)
    bias_p = np.where((c >= a)[None], coef * (rel_prev * dilation)[None], NEG_BIG).astype(np.float32)
    return jnp.asarray(bias_c), jnp.asarray(bias_p)


def _ssd_body(zs_ref, xbc_ref, dt_ref, dtbias_ref, alog_ref, dskip_ref, norm_ref, expand_ref,
              tril_ref, w_ref, o_ref, yn_ref, state_ref):
    @pl.when(pl.program_id(1) == 0)
    def _():
        state_ref[...] = jnp.zeros_like(state_ref)

    lane = lax.broadcasted_iota(jnp.int32, (SSD_CHUNK, LANES), 1)
    even = lane < HEAD_DIM
    row_i = lax.broadcasted_iota(jnp.int32, (SSD_CHUNK, SSD_CHUNK), 0)
    col_i = lax.broadcasted_iota(jnp.int32, (SSD_CHUNK, SSD_CHUNK), 1)
    causal_bias = jnp.where(row_i >= col_i, 0.0, NEG_BIG)
    zero_b = jnp.zeros((SSD_CHUNK, LANES), BF16)
    b_off = SSD_INNER
    c_off = SSD_INNER + SSD_GROUPS * SSD_STATE

    def chunk_step(ci, carry):
        rows = pl.ds(pl.multiple_of(ci * SSD_CHUNK, SSD_CHUNK), SSD_CHUNK)
        dt_pre = dt_ref[rows, :] + dtbias_ref[...]
        dt = jnp.maximum(dt_pre, 0.0) + jnp.log(1.0 + jnp.exp(-jnp.abs(dt_pre)))
        a = dt * -jnp.exp(alog_ref[...])
        a_cs = sum(_dot(tril_ref[...], t) for t in _split3(a))
        a_cs_t = a_cs.T
        dt_t = dt.T
        a_last = a_cs[SSD_CHUNK - 1:SSD_CHUNK, :]
        factors = jnp.concatenate([_split2(jnp.exp(a_cs)), _split2(jnp.exp(a_last - a_cs) * dt)], axis=0)
        for g in range(SSD_GROUPS):
            gcols = pl.ds(g * SSD_GROUP_COLS, SSD_GROUP_COLS)
            xb = xbc_ref[rows, gcols]
            x = xb.astype(F32)
            bm = xbc_ref[rows, pl.ds(b_off + g * SSD_STATE, SSD_STATE)]
            cm = xbc_ref[rows, pl.ds(c_off + g * SSD_STATE, SSD_STATE)]
            e = _dot(factors, expand_ref[:, gcols])
            e_out, e_state = e[:SSD_CHUNK], e[SSD_CHUNK:]
            cb = _dot_nt(cm, bm)
            state = state_ref[g]
            y = _dot(cm, state.astype(BF16)) * e_out
            state_ref[g] = (state * e_out[SSD_CHUNK - 1:SSD_CHUNK, :]
                            + _dot(bm.astype(F32).T.astype(BF16), (x * e_state).astype(BF16)))
            pairs = []
            for p in range(SSD_GROUP_COLS // LANES):
                lmats = []
                for hh in range(2):
                    h = g * SSD_GROUP_HEADS + 2 * p + hh
                    diff = a_cs[:, h:h + 1] - a_cs_t[h:h + 1, :] + causal_bias
                    lmats.append((cb * jnp.exp(diff) * dt_t[h:h + 1, :]).astype(BF16))
                xp = xb[:, p * LANES:(p + 1) * LANES]
                rhs = jnp.concatenate([jnp.where(even, xp, zero_b), jnp.where(even, zero_b, xp)], axis=0)
                pairs.append(_dot(jnp.concatenate(lmats, axis=1), rhs))
            y = y + jnp.concatenate(pairs, axis=1) + x * dskip_ref[:, gcols]
            y = y * zs_ref[rows, gcols].astype(F32)
            yn_ref[rows, gcols] = _rms(y, norm_ref[:, gcols]).astype(BF16)
        return carry

    lax.fori_loop(0, SSD_ROWS // SSD_CHUNK, chunk_step, 0)
    o_ref[...] = _dot(yn_ref[...], w_ref[...])


def _ssd(zs, xbc, dt, dtbias, alog, dskip, norm, expand, tril, w, batch, seq):
    assert seq % SSD_ROWS == 0
    tiles = seq // SSD_ROWS

    def row(cols):
        return pl.BlockSpec((SSD_ROWS, cols), lambda b, s: (b * tiles + s, 0))

    consts = [dtbias, alog, dskip, norm, expand, tril, w]
    return pl.pallas_call(
        _ssd_body,
        grid=(batch, tiles),
        in_specs=[row(SSD_INNER), row(SSD_CONV_DIM), row(DT_COLS)] + [_const_spec(c.shape) for c in consts],
        out_specs=row(D_MODEL),
        out_shape=jax.ShapeDtypeStruct((batch * seq, D_MODEL), F32),
        scratch_shapes=[
            pltpu.VMEM((SSD_ROWS, SSD_INNER), BF16),
            pltpu.VMEM((SSD_GROUPS, SSD_STATE, SSD_GROUP_COLS), F32),
        ],
        compiler_params=_params("parallel", "arbitrary"),
        name="ssd",
    )(zs, xbc, dt, *consts)


def _merge_body(o0_ref, o1_ref, o2_ref, l0_ref, l1_ref, l2_ref, s_ref, gate_ref, x_ref,
                expand_ref, wa_ref, wo_ref, out_ref, o_nat_ref, l_nat_ref):
    def token_order(src_ref, dst_ref, slot):
        dilation, _, cols = src_ref.shape
        if dilation == 1:
            return src_ref[0]
        slabs = cols // LANES
        for r in range(dilation):
            for j in range(slabs):
                dst_ref[slot, j, pl.ds(r, MERGE_ROWS // dilation, stride=dilation), :] = (
                    src_ref[r, :, pl.ds(j * LANES, LANES)])
        return jnp.concatenate([dst_ref[slot, j] for j in range(slabs)], axis=1)

    lses = [token_order(l_ref, l_nat_ref, k) for k, l_ref in enumerate((l0_ref, l1_ref, l2_ref))]
    m = jnp.maximum(jnp.maximum(lses[0], lses[1]), lses[2])
    es = [jnp.exp(l - m) for l in lses]
    inv = 1.0 / (es[0] + es[1] + es[2])
    attn = None
    for k, (e, o_ref) in enumerate(zip(es, (o0_ref, o1_ref, o2_ref))):
        w = _dot(_split2(e * inv), expand_ref[...])
        term = w * token_order(o_ref, o_nat_ref, k)
        attn = term if attn is None else attn + term
    a = _dot(attn.astype(BF16), wa_ref[...])
    g_attn = gate_ref[:, pl.ds(0, D_MODEL)].astype(F32)
    g_ssd = gate_ref[:, pl.ds(D_MODEL, D_MODEL)].astype(F32)
    merged = jax.nn.sigmoid(g_attn) * a + jax.nn.sigmoid(g_ssd) * s_ref[...]
    out_ref[...] = x_ref[...] + _dot(merged.astype(BF16), wo_ref[...])


def _merge(os_, lses, s, gates, x1, expand, wa, wo, batch, seq):
    assert seq % MERGE_ROWS == 0
    tiles = seq // MERGE_ROWS

    def row(cols):
        return pl.BlockSpec((MERGE_ROWS, cols), lambda b, i: (b * tiles + i, 0))

    def streams(arr):
        dilation, cols = arr.shape[1], arr.shape[3]
        return pl.BlockSpec((None, dilation, MERGE_ROWS // dilation, cols), lambda b, i: (b, 0, i, 0))

    n = len(ATTN_PATTERNS)
    return pl.pallas_call(
        _merge_body,
        grid=(batch, tiles),
        in_specs=[streams(o) for o in os_] + [streams(l) for l in lses]
                 + [row(D_MODEL), row(2 * D_MODEL), row(D_MODEL),
                    _const_spec(expand.shape), _const_spec(wa.shape), _const_spec(wo.shape)],
        out_specs=row(D_MODEL),
        out_shape=jax.ShapeDtypeStruct((batch * seq, D_MODEL), F32),
        scratch_shapes=[pltpu.VMEM((n, ATTN_OUT // LANES, MERGE_ROWS, LANES), F32),
                        pltpu.VMEM((n, 1, MERGE_ROWS, LANES), F32)],
        compiler_params=_params("parallel", "parallel"),
        name="merge",
    )(*os_, *lses, s, gates, x1, expand, wa, wo)


def _head_expand(n_heads, rows):
    e = np.zeros((rows, n_heads * HEAD_DIM), np.float32)
    for h in range(n_heads):
        e[h, h * HEAD_DIM:(h + 1) * HEAD_DIM] = 1.0
    return jnp.asarray(np.concatenate([e, e], axis=0), BF16)


def _pad_lanes(v, width):
    return jnp.pad(v.astype(F32), (0, width - v.shape[0]))[None, :]


def kernel(x, ffn1_norm, ffn1_w_gate, ffn1_w_up, ffn1_w_down, mix_norm, w_in, q_norm, k_norm, conv_w, conv_b, dt_bias, a_log, d_skip, ssd_norm, w_attn_branch, w_ssd_branch, w_out, ffn2_norm, ffn2_w_gate, ffn2_w_up, ffn2_w_down):
    batch, seq, _ = x.shape
    depth = ffn1_norm.shape[0]
    x2d = x.reshape(batch * seq, D_MODEL)

    headmean = np.zeros((MXU_DIM, MXU_DIM), np.float32)
    for h in range(MXU_DIM // HEAD_DIM):
        headmean[h * HEAD_DIM:(h + 1) * HEAD_DIM, h * HEAD_DIM:(h + 1) * HEAD_DIM] = 1.0 / HEAD_DIM
    headmean = jnp.asarray(headmean, BF16)
    tril = jnp.asarray(np.tril(np.ones((SSD_CHUNK, SSD_CHUNK), np.float32)), BF16)
    expand_ssd = _head_expand(SSD_HEADS, LANES)
    expand_attn = _head_expand(HEADS_PER_PATTERN, LANES)
    biases = [_attn_bias(g, d) for g, (_, d) in enumerate(ATTN_PATTERNS)]

    for l in range(depth):
        x2d = _ffn(x2d, ffn1_norm[l][None, :], ffn1_w_gate[l].astype(BF16), ffn1_w_up[l].astype(BF16),
                   ffn1_w_down[l].astype(BF16))

        w = w_in[l]
        n_dt = SSD_HEADS
        qkv_end = 3 * ATTN_QKV
        dt_start = qkv_end + SSD_INNER + SSD_CONV_DIM
        per_pattern = [w[:, part * ATTN_QKV + g * ATTN_OUT: part * ATTN_QKV + (g + 1) * ATTN_OUT]
                       for g in range(len(ATTN_PATTERNS)) for part in range(3)]
        w_packed = jnp.concatenate(
            per_pattern + [w[:, qkv_end:dt_start],
                           jnp.pad(w[:, dt_start:dt_start + n_dt], ((0, 0), (0, DT_COLS - n_dt))),
                           w[:, dt_start + n_dt:]], axis=1).astype(BF16)
        reps = PROJ_CHUNK // HEAD_DIM
        qgain = jnp.tile(q_norm[l].astype(F32) * (1.0 / math.sqrt(HEAD_DIM)), reps)[None, :]
        kgain = jnp.tile(k_norm[l].astype(F32), reps)[None, :]
        qkv0, qkv1, qkv2, zs, xbc, dt, gates = _proj(
            x2d, mix_norm[l][None, :], w_packed, qgain, kgain, headmean,
            conv_w[l].astype(F32), conv_b[l].astype(F32)[None, :], batch, seq)

        os_, lses = [], []
        for g, qkv in enumerate((qkv0.reshape(batch, 1, seq, PATTERN_COLS), qkv1, qkv2)):
            o, lse = _attn_pattern(qkv, biases[g][0], biases[g][1])
            os_.append(o)
            lses.append(lse)

        s = _ssd(zs, xbc, dt, _pad_lanes(dt_bias[l], DT_COLS), _pad_lanes(a_log[l], DT_COLS),
                 jnp.repeat(d_skip[l].astype(F32), HEAD_DIM)[None, :], ssd_norm[l].astype(F32)[None, :],
                 expand_ssd, tril, w_ssd_branch[l].astype(BF16), batch, seq)

        x2d = _merge(os_, lses, s, gates, x2d, expand_attn, w_attn_branch[l].astype(BF16),
                     w_out[l].astype(BF16), batch, seq)

        x2d = _ffn(x2d, ffn2_norm[l][None, :], ffn2_w_gate[l].astype(BF16), ffn2_w_up[l].astype(BF16),
                   ffn2_w_down[l].astype(BF16))
    return x2d.reshape(batch, seq, D_MODEL)
```

```python
import math

import jax
import jax.numpy as jnp
import numpy as np
from jax import lax
from jax.experimental import pallas as pl
from jax.experimental.pallas import tpu as pltpu

F32 = jnp.float32
BF16 = jnp.bfloat16

D_MODEL = 1024
HEAD_DIM = 64
ATTN_PATTERNS = ((128, 1), (512, 4), (2048, 16))
HEADS_PER_PATTERN = 8
N_ATTN_HEADS = HEADS_PER_PATTERN * len(ATTN_PATTERNS)
ATTN_QKV = N_ATTN_HEADS * HEAD_DIM
ATTN_OUT = HEADS_PER_PATTERN * HEAD_DIM
ATTN_BLOCK = 128
ALIBI_MAX_EXP = 8.0
SSD_INNER = 2048
SSD_HEADS = 32
SSD_STATE = 128
SSD_GROUPS = 4
SSD_CONV = 4
SSD_CHUNK = 128
SSD_GROUP_COLS = SSD_INNER // SSD_GROUPS
SSD_GROUP_HEADS = SSD_HEADS // SSD_GROUPS
SSD_CONV_DIM = SSD_INNER + 2 * SSD_GROUPS * SSD_STATE
D_FF = 2816
EPS = 1e-6

LANES = 128
SUBLANES = 8
VMEM_LIMIT_BYTES = 56 * 1024 * 1024

FFN_ROWS = 512
FFN_CHUNK = 256
PROJ_ROWS = 512
PROJ_CHUNK = 256
ATTN_ROWS = 256
SSD_ROWS = 512
MERGE_ROWS = 512
DT_COLS = LANES

PATTERN_COLS = 3 * ATTN_OUT
ATTN_PROJ_COLS = len(ATTN_PATTERNS) * PATTERN_COLS
OFF_Z = 0
OFF_XBC = OFF_Z + SSD_INNER
OFF_DT = OFF_XBC + SSD_CONV_DIM
OFF_GATE = OFF_DT + DT_COLS
SSD_PROJ_COLS = OFF_GATE + 2 * D_MODEL
LOG2E = math.log2(math.e)
LN2 = math.log(2.0)

NEG_BIG = -1e30


def _const_spec(shape):
    nd = len(shape)
    return pl.BlockSpec(shape, lambda *_: (0,) * nd, pipeline_mode=pl.Buffered(1))


def _params(*sem):
    return pltpu.CompilerParams(dimension_semantics=sem, vmem_limit_bytes=VMEM_LIMIT_BYTES)


def _rms(x, gain):
    ms = jnp.mean(x * x, axis=-1, keepdims=True)
    return x * lax.rsqrt(ms + EPS) * gain


def _silu(x):
    return x * jax.nn.sigmoid(x)


def _dot(a, b):
    return jnp.dot(a, b, preferred_element_type=F32)


def _dot_nt(a, b):
    return lax.dot_general(a, b, (((1,), (1,)), ((), ())), preferred_element_type=F32)


def _split2(x):
    hi = x.astype(BF16)
    lo = (x - hi.astype(F32)).astype(BF16)
    return jnp.concatenate([hi, lo], axis=1)


def _split3(x):
    a = x.astype(BF16)
    r = x - a.astype(F32)
    b = r.astype(BF16)
    c = (r - b.astype(F32)).astype(BF16)
    return a, b, c


def _ffn_body(x_ref, gain_ref, wg_ref, wu_ref, wd_ref, o_ref, act_ref):
    x = x_ref[...]
    h = _rms(x, gain_ref[...]).astype(BF16)
    for c in range(D_FF // FFN_CHUNK):
        cs = pl.ds(c * FFN_CHUNK, FFN_CHUNK)
        g = _dot(h, wg_ref[:, cs])
        u = _dot(h, wu_ref[:, cs])
        act_ref[:, cs] = (_silu(g) * u).astype(BF16)
    y = _dot(act_ref[...], wd_ref[...])
    o_ref[...] = x + 0.5 * y


def _ffn(x2d, gain, wg, wu, wd):
    t = x2d.shape[0]
    assert t % FFN_ROWS == 0
    row = pl.BlockSpec((FFN_ROWS, D_MODEL), lambda i: (i, 0))
    return pl.pallas_call(
        _ffn_body,
        grid=(t // FFN_ROWS,),
        in_specs=[row, _const_spec((1, D_MODEL)), _const_spec((D_MODEL, D_FF)),
                  _const_spec((D_MODEL, D_FF)), _const_spec((D_FF, D_MODEL))],
        out_specs=row,
        out_shape=jax.ShapeDtypeStruct((t, D_MODEL), F32),
        scratch_shapes=[pltpu.VMEM((FFN_ROWS, D_FF), BF16)],
        compiler_params=_params("parallel"),
        name="ffn",
    )(x2d, gain, wg, wu, wd)


def _run_interleaved(*item_lists):
    keyed = [((i + 0.5) / len(items), rank, i, item)
             for rank, items in enumerate(item_lists) for i, item in enumerate(items)]
    pending = None
    for _, _, _, (start, finish, *args) in sorted(keyed, key=lambda k: k[:3]):
        y = start(*args)
        if pending is not None:
            pending[0](*pending[1], pending[2])
        pending = (finish, args, y)
    pending[0](*pending[1], pending[2])


def _proj_attn_body(x_ref, gain_ref, w_ref, qgain_ref, kgain_ref, headmean_ref,
                    qkv0_ref, qkv1_ref, qkv2_ref, slab_ref, h_ref):
    gain = gain_ref[...]

    def normed(rows):
        return _rms(rows, gain).astype(BF16)

    h_ref[0] = normed(x_ref[...])
    for j in range(D_MODEL // LANES):
        slab_ref[j] = x_ref[:, pl.ds(j * LANES, LANES)]
    for g, (_, dilation) in enumerate(ATTN_PATTERNS):
        if dilation > 1:
            per_stream = PROJ_ROWS // dilation
            h_ref[g] = normed(jnp.concatenate(
                [jnp.concatenate([slab_ref[j, pl.ds(r, per_stream, stride=dilation), :]
                                  for j in range(D_MODEL // LANES)], axis=1)
                 for r in range(dilation)], axis=0))

    def start(g, part, sub):
        col0 = part * ATTN_OUT + sub * PROJ_CHUNK
        return _dot(h_ref[g], w_ref[:, pl.ds(g * PATTERN_COLS + col0, PROJ_CHUNK)])

    def finish(g, part, sub, y):
        dilation = ATTN_PATTERNS[g][1]
        out_ref = (qkv0_ref, qkv1_ref, qkv2_ref)[g]
        per_stream = PROJ_ROWS // dilation
        if part < 2:
            ms = _dot((y * y).astype(BF16), headmean_ref[...])
            y = y * lax.rsqrt(ms + EPS) * (qgain_ref[...] if part == 0 else kgain_ref[...])
        y = y.astype(BF16)
        cols = pl.ds(part * ATTN_OUT + sub * PROJ_CHUNK, PROJ_CHUNK)
        if dilation == 1:
            out_ref[:, cols] = y
        else:
            for r in range(dilation):
                out_ref[r, :, cols] = y[r * per_stream:(r + 1) * per_stream]

    subs = ATTN_OUT // PROJ_CHUNK
    n_pat = len(ATTN_PATTERNS)
    _run_interleaved([(start, finish, g, 2, sub) for g in range(n_pat) for sub in range(subs)],
                     [(start, finish, g, part, sub)
                      for g in range(n_pat) for part in range(2) for sub in range(subs)])


def _proj_ssd_body(x_ref, gain_ref, w_ref, convw_ref, convb_ref,
                   z_ref, xbc_ref, dt_ref, gate_ref, h_ref, *ext_refs):
    @pl.when(pl.program_id(1) == 0)
    def _():
        for ext_ref in ext_refs:
            ext_ref[pl.ds(0, SUBLANES), :] = jnp.zeros((SUBLANES, PROJ_CHUNK), F32)

    h_ref[...] = _rms(x_ref[...], gain_ref[...]).astype(BF16)

    def matmul(off, c):
        return _dot(h_ref[...], w_ref[:, pl.ds(off + c * PROJ_CHUNK, PROJ_CHUNK)])

    def z_finish(off, c, y):
        z_ref[:, pl.ds(c * PROJ_CHUNK, PROJ_CHUNK)] = _silu(y).astype(BF16)

    def xbc_finish(off, c, y):
        cs = pl.ds(c * PROJ_CHUNK, PROJ_CHUNK)
        ext_ref = ext_refs[c]
        ext_ref[pl.ds(SUBLANES, PROJ_ROWS), :] = y
        acc = convb_ref[:, cs]
        for back in range(SSD_CONV):
            acc = acc + convw_ref[pl.ds(SSD_CONV - 1 - back, 1), cs] * ext_ref[pl.ds(SUBLANES - back, PROJ_ROWS), :]
        xbc_ref[:, cs] = _silu(acc).astype(BF16)
        ext_ref[pl.ds(0, SUBLANES), :] = ext_ref[pl.ds(PROJ_ROWS, SUBLANES), :]

    def gate_finish(off, c, y):
        gate_ref[:, pl.ds(c * PROJ_CHUNK, PROJ_CHUNK)] = y.astype(BF16)

    def dt_start():
        return _dot(h_ref[...], w_ref[:, pl.ds(OFF_DT, DT_COLS)])

    def dt_finish(y):
        dt_ref[...] = y

    _run_interleaved([(matmul, gate_finish, OFF_GATE, c) for c in range(2 * D_MODEL // PROJ_CHUNK)]
                     + [(dt_start, dt_finish)],
                     [(matmul, xbc_finish, OFF_XBC, c) for c in range(SSD_CONV_DIM // PROJ_CHUNK)],
                     [(matmul, z_finish, OFF_Z, c) for c in range(SSD_INNER // PROJ_CHUNK)])


def _proj(x2d, gain, w_attn, w_ssd, qgain, kgain, headmean, convw, convb, batch, seq):
    assert seq % PROJ_ROWS == 0
    tiles = seq // PROJ_ROWS
    t = batch * seq

    def row(cols):
        return pl.BlockSpec((PROJ_ROWS, cols), lambda b, i: (b * tiles + i, 0))

    def streams(dilation):
        return pl.BlockSpec((None, dilation, PROJ_ROWS // dilation, PATTERN_COLS), lambda b, i: (b, 0, i, 0))

    def stream_shape(dilation):
        return jax.ShapeDtypeStruct((batch, dilation, seq // dilation, PATTERN_COLS), BF16)

    d1, d2 = ATTN_PATTERNS[1][1], ATTN_PATTERNS[2][1]
    qkv = pl.pallas_call(
        _proj_attn_body,
        grid=(batch, tiles),
        in_specs=[row(D_MODEL), _const_spec((1, D_MODEL)), _const_spec((D_MODEL, ATTN_PROJ_COLS)),
                  _const_spec((1, PROJ_CHUNK)), _const_spec((1, PROJ_CHUNK)),
                  _const_spec((PROJ_CHUNK, PROJ_CHUNK))],
        out_specs=[row(PATTERN_COLS), streams(d1), streams(d2)],
        out_shape=[jax.ShapeDtypeStruct((t, PATTERN_COLS), BF16), stream_shape(d1), stream_shape(d2)],
        scratch_shapes=[pltpu.VMEM((D_MODEL // LANES, PROJ_ROWS, LANES), F32),
                        pltpu.VMEM((len(ATTN_PATTERNS), PROJ_ROWS, D_MODEL), BF16)],
        compiler_params=_params("parallel", "parallel"),
        name="proj_attn",
    )(x2d, gain, w_attn, qgain, kgain, headmean)
    ssd_in = pl.pallas_call(
        _proj_ssd_body,
        grid=(batch, tiles),
        in_specs=[row(D_MODEL), _const_spec((1, D_MODEL)), _const_spec((D_MODEL, SSD_PROJ_COLS)),
                  _const_spec(convw.shape), _const_spec(convb.shape)],
        out_specs=[row(SSD_INNER), row(SSD_CONV_DIM), row(DT_COLS), row(2 * D_MODEL)],
        out_shape=[jax.ShapeDtypeStruct((t, SSD_INNER), BF16),
                   jax.ShapeDtypeStruct((t, SSD_CONV_DIM), BF16),
                   jax.ShapeDtypeStruct((t, DT_COLS), F32),
                   jax.ShapeDtypeStruct((t, 2 * D_MODEL), BF16)],
        scratch_shapes=[pltpu.VMEM((PROJ_ROWS, D_MODEL), BF16)]
                       + [pltpu.VMEM((SUBLANES + PROJ_ROWS, PROJ_CHUNK), F32)] * (SSD_CONV_DIM // PROJ_CHUNK),
        compiler_params=_params("parallel", "arbitrary"),
        name="proj_ssd",
    )(x2d, gain, w_ssd, convw, convb)
    return (*qkv, *ssd_in)


def _attn_body(q_ref, kc_ref, kp_ref, vc_ref, vp_ref, bias_ref, ones_ref, o_ref, lse_ref):
    first = jnp.where(pl.program_id(2) == 0, 1, 0)
    lane = lax.broadcasted_iota(jnp.int32, (ATTN_BLOCK, LANES), 1)
    even = lane < HEAD_DIM
    lane2 = lax.broadcasted_iota(jnp.int32, (2 * ATTN_BLOCK, LANES), 1)
    even2 = lane2 < HEAD_DIM
    zero = jnp.zeros((ATTN_BLOCK, LANES), BF16)
    zero2 = jnp.zeros((2 * ATTN_BLOCK, LANES), BF16)
    for i in range(ATTN_ROWS // ATTN_BLOCK):
        rows = pl.ds(i * ATTN_BLOCK, ATTN_BLOCK)
        table = first if i == 0 else 0
        lse_tile = jnp.zeros((ATTN_BLOCK, LANES), F32)
        for p in range(HEADS_PER_PATTERN // 2):
            cols = pl.ds(p * LANES, LANES)
            q2 = q_ref[rows, cols]
            if i == 0:
                kk = jnp.concatenate([kp_ref[:, cols], kc_ref[rows, cols]], axis=0)
                vv = jnp.concatenate([vp_ref[:, cols], vc_ref[rows, cols]], axis=0)
            else:
                both = pl.ds((i - 1) * ATTN_BLOCK, 2 * ATTN_BLOCK)
                kk, vv = kc_ref[both, cols], vc_ref[both, cols]
            qq = jnp.concatenate([jnp.where(even, q2, zero), jnp.where(even, zero, q2)], axis=0)
            s = _dot_nt(qq, kk)
            probs, m_pair = [], []
            for hh in range(2):
                sh = s[hh * ATTN_BLOCK:(hh + 1) * ATTN_BLOCK] + bias_ref[table, 2 * p + hh]
                m = jnp.max(jnp.maximum(sh[:, :ATTN_BLOCK], sh[:, ATTN_BLOCK:]), axis=-1, keepdims=True)
                probs.append(jnp.exp2(sh - m).astype(BF16))
                m_pair.append(m)
            vmat = jnp.concatenate([jnp.where(even2, vv, zero2), jnp.where(even2, zero2, vv)], axis=0)
            o_ext = _dot(jnp.concatenate(probs, axis=1), jnp.concatenate([vmat, ones_ref[...]], axis=1))
            l_pair = o_ext[:, LANES:]
            o_ref[rows, cols] = o_ext[:, :LANES] / l_pair
            lse_pair = (jnp.where(even, m_pair[0], m_pair[1]) + jnp.log2(l_pair)) * LN2
            lse_tile = jnp.where((lane == p) | (lane == HEAD_DIM + p), lse_pair, lse_tile)
        lse_ref[rows, :] = lse_tile


def _attn_pattern(qkv, bias, ones):
    batch, dilation, stream_len, _ = qkv.shape
    assert stream_len % ATTN_ROWS == 0
    sub = ATTN_ROWS // ATTN_BLOCK

    def cur(part):
        return pl.BlockSpec((None, None, ATTN_ROWS, ATTN_OUT), lambda b, r, n: (b, r, n, part))

    def prev(part):
        return pl.BlockSpec((None, None, ATTN_BLOCK, ATTN_OUT),
                            lambda b, r, n: (b, r, jnp.maximum(n * sub - 1, 0), part))

    return pl.pallas_call(
        _attn_body,
        grid=(batch, dilation, stream_len // ATTN_ROWS),
        in_specs=[cur(0), cur(1), prev(1), cur(2), prev(2), _const_spec(bias.shape), _const_spec(ones.shape)],
        out_specs=[pl.BlockSpec((None, None, ATTN_ROWS, ATTN_OUT), lambda b, r, n: (b, r, n, 0)),
                   pl.BlockSpec((None, None, ATTN_ROWS, LANES), lambda b, r, n: (b, r, n, 0))],
        out_shape=[jax.ShapeDtypeStruct((batch, dilation, stream_len, ATTN_OUT), F32),
                   jax.ShapeDtypeStruct((batch, dilation, stream_len, LANES), F32)],
        compiler_params=_params("parallel", "parallel", "parallel"),
        name=f"attn_d{dilation}",
    )(qkv, qkv, qkv, qkv, qkv, bias, ones)


def _attn_bias(group, dilation):
    h = np.arange(group * HEADS_PER_PATTERN, (group + 1) * HEADS_PER_PATTERN, dtype=np.float64)
    slopes = np.exp2(-ALIBI_MAX_EXP * (h + 1) / N_ATTN_HEADS).astype(np.float32)
    a = np.arange(ATTN_BLOCK)[:, None]
    c = np.arange(ATTN_BLOCK)[None, :]
    rel_cur = (a - c).astype(np.float32)
    rel_prev = rel_cur + ATTN_BLOCK
    coef = -slopes[:, None, None] * np.float32(LOG2E)
    bias_c = np.where((a >= c)[None], coef * (rel_cur * dilation)[None], NEG_BIG)
    bias_p = np.where((c >= a)[None], coef * (rel_prev * dilation)[None], NEG_BIG)
    table = np.stack([np.concatenate([bias_p, bias_c], axis=2),
                      np.concatenate([np.full_like(bias_p, NEG_BIG), bias_c], axis=2)])
    return jnp.asarray(table.astype(np.float32))


def _attn_ones():
    ones = np.zeros((4 * ATTN_BLOCK, LANES), np.float32)
    ones[:2 * ATTN_BLOCK, :HEAD_DIM] = 1.0
    ones[2 * ATTN_BLOCK:, HEAD_DIM:] = 1.0
    return jnp.asarray(ones, BF16)


def _ssd_body(zs_ref, xbc_ref, dt_ref, dtbias_ref, alog_ref, dskip_ref, norm_ref, expand_ref,
              tril_ref, w_ref, o_ref, yn_ref, state_ref):
    @pl.when(pl.program_id(1) == 0)
    def _():
        state_ref[...] = jnp.zeros_like(state_ref)

    lane = lax.broadcasted_iota(jnp.int32, (SSD_CHUNK, LANES), 1)
    even = lane < HEAD_DIM
    row_i = lax.broadcasted_iota(jnp.int32, (SSD_CHUNK, SSD_CHUNK), 0)
    col_i = lax.broadcasted_iota(jnp.int32, (SSD_CHUNK, SSD_CHUNK), 1)
    causal_bias = jnp.where(row_i >= col_i, 0.0, NEG_BIG)
    zero_b = jnp.zeros((SSD_CHUNK, LANES), BF16)
    b_off = SSD_INNER
    c_off = SSD_INNER + SSD_GROUPS * SSD_STATE

    def chunk_step(ci, carry):
        rows = pl.ds(pl.multiple_of(ci * SSD_CHUNK, SSD_CHUNK), SSD_CHUNK)
        dt_pre = dt_ref[rows, :] + dtbias_ref[...]
        dt = jnp.maximum(dt_pre, 0.0) + jnp.log(1.0 + jnp.exp(-jnp.abs(dt_pre)))
        a = dt * -jnp.exp(alog_ref[...])
        a_cs = sum(_dot(tril_ref[...], t) for t in _split3(a))
        a_cs_t = a_cs.T
        dt_t = dt.T
        a_last = a_cs[SSD_CHUNK - 1:SSD_CHUNK, :]
        factors = jnp.concatenate([_split2(jnp.exp(a_cs)), _split2(jnp.exp(a_last - a_cs) * dt)], axis=0)
        for g in range(SSD_GROUPS):
            gcols = pl.ds(g * SSD_GROUP_COLS, SSD_GROUP_COLS)
            xb = xbc_ref[rows, gcols]
            x = xb.astype(F32)
            bm = xbc_ref[rows, pl.ds(b_off + g * SSD_STATE, SSD_STATE)]
            cm = xbc_ref[rows, pl.ds(c_off + g * SSD_STATE, SSD_STATE)]
            e = _dot(factors, expand_ref[:, gcols])
            e_out, e_state = e[:SSD_CHUNK], e[SSD_CHUNK:]
            cb = _dot_nt(cm, bm)
            state = state_ref[g]
            y = _dot(cm, state.astype(BF16)) * e_out
            state_ref[g] = (state * e_out[SSD_CHUNK - 1:SSD_CHUNK, :]
                            + _dot(bm.astype(F32).T.astype(BF16), (x * e_state).astype(BF16)))
            pairs = []
            for p in range(SSD_GROUP_COLS // LANES):
                lmats = []
                for hh in range(2):
                    h = g * SSD_GROUP_HEADS + 2 * p + hh
                    diff = a_cs[:, h:h + 1] - a_cs_t[h:h + 1, :] + causal_bias
                    lmats.append((cb * jnp.exp(diff) * dt_t[h:h + 1, :]).astype(BF16))
                xp = xb[:, p * LANES:(p + 1) * LANES]
                rhs = jnp.concatenate([jnp.where(even, xp, zero_b), jnp.where(even, zero_b, xp)], axis=0)
                pairs.append(_dot(jnp.concatenate(lmats, axis=1), rhs))
            y = y + jnp.concatenate(pairs, axis=1) + x * dskip_ref[:, gcols]
            y = y * zs_ref[rows, gcols].astype(F32)
            yn_ref[rows, gcols] = _rms(y, norm_ref[:, gcols]).astype(BF16)
        return carry

    lax.fori_loop(0, SSD_ROWS // SSD_CHUNK, chunk_step, 0)
    o_ref[...] = _dot(yn_ref[...], w_ref[...])


def _ssd(zs, xbc, dt, dtbias, alog, dskip, norm, expand, tril, w, batch, seq):
    assert seq % SSD_ROWS == 0
    tiles = seq // SSD_ROWS

    def row(cols):
        return pl.BlockSpec((SSD_ROWS, cols), lambda b, s: (b * tiles + s, 0))

    consts = [dtbias, alog, dskip, norm, expand, tril, w]
    return pl.pallas_call(
        _ssd_body,
        grid=(batch, tiles),
        in_specs=[row(SSD_INNER), row(SSD_CONV_DIM), row(DT_COLS)] + [_const_spec(c.shape) for c in consts],
        out_specs=row(D_MODEL),
        out_shape=jax.ShapeDtypeStruct((batch * seq, D_MODEL), F32),
        scratch_shapes=[
            pltpu.VMEM((SSD_ROWS, SSD_INNER), BF16),
            pltpu.VMEM((SSD_GROUPS, SSD_STATE, SSD_GROUP_COLS), F32),
        ],
        compiler_params=_params("parallel", "arbitrary"),
        name="ssd",
    )(zs, xbc, dt, *consts)


def _merge_body(o0_ref, o1_ref, o2_ref, l0_ref, l1_ref, l2_ref, s_ref, gate_ref, x_ref,
                expand_ref, wa_ref, wo_ref, out_ref, o_nat_ref, l_nat_ref):
    def token_order(src_ref, dst_ref, slot):
        dilation, _, cols = src_ref.shape
        if dilation == 1:
            return src_ref[0]
        slabs = cols // LANES
        for r in range(dilation):
            for j in range(slabs):
                dst_ref[slot, j, pl.ds(r, MERGE_ROWS // dilation, stride=dilation), :] = (
                    src_ref[r, :, pl.ds(j * LANES, LANES)])
        return jnp.concatenate([dst_ref[slot, j] for j in range(slabs)], axis=1)

    lses = [token_order(l_ref, l_nat_ref, k) for k, l_ref in enumerate((l0_ref, l1_ref, l2_ref))]
    m = jnp.maximum(jnp.maximum(lses[0], lses[1]), lses[2])
    es = [jnp.exp(l - m) for l in lses]
    inv = 1.0 / (es[0] + es[1] + es[2])
    attn = None
    for k, (e, o_ref) in enumerate(zip(es, (o0_ref, o1_ref, o2_ref))):
        w = _dot(_split2(e * inv), expand_ref[...])
        term = w * token_order(o_ref, o_nat_ref, k)
        attn = term if attn is None else attn + term
    a = _dot(attn.astype(BF16), wa_ref[...])
    g_attn = gate_ref[:, pl.ds(0, D_MODEL)].astype(F32)
    g_ssd = gate_ref[:, pl.ds(D_MODEL, D_MODEL)].astype(F32)
    merged = jax.nn.sigmoid(g_attn) * a + jax.nn.sigmoid(g_ssd) * s_ref[...]
    out_ref[...] = x_ref[...] + _dot(merged.astype(BF16), wo_ref[...])


def _merge(os_, lses, s, gates, x1, expand, wa, wo, batch, seq):
    assert seq % MERGE_ROWS == 0
    tiles = seq // MERGE_ROWS

    def row(cols):
        return pl.BlockSpec((MERGE_ROWS, cols), lambda b, i: (b * tiles + i, 0))

    def streams(arr):
        dilation, cols = arr.shape[1], arr.shape[3]
        return pl.BlockSpec((None, dilation, MERGE_ROWS // dilation, cols), lambda b, i: (b, 0, i, 0))

    n = len(ATTN_PATTERNS)
    return pl.pallas_call(
        _merge_body,
        grid=(batch, tiles),
        in_specs=[streams(o) for o in os_] + [streams(l) for l in lses]
                 + [row(D_MODEL), row(2 * D_MODEL), row(D_MODEL),
                    _const_spec(expand.shape), _const_spec(wa.shape), _const_spec(wo.shape)],
        out_specs=row(D_MODEL),
        out_shape=jax.ShapeDtypeStruct((batch * seq, D_MODEL), F32),
        scratch_shapes=[pltpu.VMEM((n, ATTN_OUT // LANES, MERGE_ROWS, LANES), F32),
                        pltpu.VMEM((n, 1, MERGE_ROWS, LANES), F32)],
        compiler_params=_params("parallel", "parallel"),
        name="merge",
    )(*os_, *lses, s, gates, x1, expand, wa, wo)


def _head_expand(lane_heads, n_heads):
    e = np.zeros((LANES, n_heads * HEAD_DIM), np.float32)
    for i, h in enumerate(lane_heads):
        if h is not None:
            e[i, h * HEAD_DIM:(h + 1) * HEAD_DIM] = 1.0
    return jnp.asarray(np.concatenate([e, e], axis=0), BF16)


def _pad_lanes(v, width):
    return jnp.pad(v.astype(F32), (0, width - v.shape[0]))[None, :]


def kernel(x, ffn1_norm, ffn1_w_gate, ffn1_w_up, ffn1_w_down, mix_norm, w_in, q_norm, k_norm, conv_w, conv_b, dt_bias, a_log, d_skip, ssd_norm, w_attn_branch, w_ssd_branch, w_out, ffn2_norm, ffn2_w_gate, ffn2_w_up, ffn2_w_down):
    batch, seq, _ = x.shape
    depth = ffn1_norm.shape[0]
    x2d = x.reshape(batch * seq, D_MODEL)

    headmean = np.zeros((PROJ_CHUNK, PROJ_CHUNK), np.float32)
    for h in range(PROJ_CHUNK // HEAD_DIM):
        headmean[h * HEAD_DIM:(h + 1) * HEAD_DIM, h * HEAD_DIM:(h + 1) * HEAD_DIM] = 1.0 / HEAD_DIM
    headmean = jnp.asarray(headmean, BF16)
    tril = jnp.asarray(np.tril(np.ones((SSD_CHUNK, SSD_CHUNK), np.float32)), BF16)
    expand_ssd = _head_expand(list(range(SSD_HEADS)), SSD_HEADS)
    pairs = HEADS_PER_PATTERN // 2
    expand_attn = _head_expand(list(range(0, HEADS_PER_PATTERN, 2)) + [None] * (HEAD_DIM - pairs)
                               + list(range(1, HEADS_PER_PATTERN, 2)), HEADS_PER_PATTERN)
    biases = [_attn_bias(g, d) for g, (_, d) in enumerate(ATTN_PATTERNS)]
    ones = _attn_ones()

    for l in range(depth):
        x2d = _ffn(x2d, ffn1_norm[l][None, :], ffn1_w_gate[l].astype(BF16), ffn1_w_up[l].astype(BF16),
                   ffn1_w_down[l].astype(BF16))

        w = w_in[l]
        n_dt = SSD_HEADS
        qkv_end = 3 * ATTN_QKV
        dt_start = qkv_end + SSD_INNER + SSD_CONV_DIM
        w_attn = jnp.concatenate(
            [w[:, part * ATTN_QKV + g * ATTN_OUT: part * ATTN_QKV + (g + 1) * ATTN_OUT]
             for g in range(len(ATTN_PATTERNS)) for part in range(3)], axis=1).astype(BF16)
        w_ssd = jnp.concatenate(
            [w[:, qkv_end:dt_start], jnp.pad(w[:, dt_start:dt_start + n_dt], ((0, 0), (0, DT_COLS - n_dt))),
             w[:, dt_start + n_dt:]], axis=1).astype(BF16)
        reps = PROJ_CHUNK // HEAD_DIM
        qgain = jnp.tile(q_norm[l].astype(F32) * (LOG2E / math.sqrt(HEAD_DIM)), reps)[None, :]
        kgain = jnp.tile(k_norm[l].astype(F32), reps)[None, :]
        qkv0, qkv1, qkv2, zs, xbc, dt, gates = _proj(
            x2d, mix_norm[l][None, :], w_attn, w_ssd, qgain, kgain, headmean,
            conv_w[l].astype(F32), conv_b[l].astype(F32)[None, :], batch, seq)

        os_, lses = [], []
        for g, qkv in enumerate((qkv0.reshape(batch, 1, seq, PATTERN_COLS), qkv1, qkv2)):
            o, lse = _attn_pattern(qkv, biases[g], ones)
            os_.append(o)
            lses.append(lse)

        s = _ssd(zs, xbc, dt, _pad_lanes(dt_bias[l], DT_COLS), _pad_lanes(a_log[l], DT_COLS),
                 jnp.repeat(d_skip[l].astype(F32), HEAD_DIM)[None, :], ssd_norm[l].astype(F32)[None, :],
                 expand_ssd, tril, w_ssd_branch[l].astype(BF16), batch, seq)

        x2d = _merge(os_, lses, s, gates, x2d, expand_attn, w_attn_branch[l].astype(BF16),
                     w_out[l].astype(BF16), batch, seq)

        x2d = _ffn(x2d, ffn2_norm[l][None, :], ffn2_w_gate[l].astype(BF16), ffn2_w_up[l].astype(BF16),
                   ffn2_w_down[l].astype(BF16))
    return x2d.reshape(batch, seq, D_MODEL)
```

```python
import math

import jax
import jax.numpy as jnp
import numpy as np
from jax import lax
from jax.experimental import pallas as pl
from jax.experimental.pallas import tpu as pltpu

F32 = jnp.float32
BF16 = jnp.bfloat16

D_MODEL = 1024
HEAD_DIM = 64
ATTN_PATTERNS = ((128, 1), (512, 4), (2048, 16))
HEADS_PER_PATTERN = 8
N_ATTN_HEADS = HEADS_PER_PATTERN * len(ATTN_PATTERNS)
ATTN_QKV = N_ATTN_HEADS * HEAD_DIM
ATTN_OUT = HEADS_PER_PATTERN * HEAD_DIM
ATTN_BLOCK = 128
ALIBI_MAX_EXP = 8.0
SSD_INNER = 2048
SSD_HEADS = 32
SSD_STATE = 128
SSD_GROUPS = 4
SSD_CONV = 4
SSD_CHUNK = 128
SSD_GROUP_COLS = SSD_INNER // SSD_GROUPS
SSD_GROUP_HEADS = SSD_HEADS // SSD_GROUPS
SSD_CONV_DIM = SSD_INNER + 2 * SSD_GROUPS * SSD_STATE
D_FF = 2816
EPS = 1e-6

LANES = 128
SUBLANES = 8
VMEM_LIMIT_BYTES = 56 * 1024 * 1024

FFN_ROWS = 512
FFN_CHUNK = 256
PROJ_ROWS = 512
PROJ_CHUNK = 256
ATTN_ROWS = 512
SSD_ROWS = 512
MERGE_ROWS = 512
DT_COLS = LANES

PATTERN_COLS = 3 * ATTN_OUT
ATTN_PROJ_COLS = len(ATTN_PATTERNS) * PATTERN_COLS
OFF_Z = 0
OFF_XBC = OFF_Z + SSD_INNER
OFF_DT = OFF_XBC + SSD_CONV_DIM
OFF_GATE = OFF_DT + DT_COLS
SSD_PROJ_COLS = OFF_GATE + 2 * D_MODEL
LOG2E = math.log2(math.e)
LN2 = math.log(2.0)

NEG_BIG = -1e30


def _const_spec(shape):
    nd = len(shape)
    return pl.BlockSpec(shape, lambda *_: (0,) * nd, pipeline_mode=pl.Buffered(1))


def _params(*sem):
    return pltpu.CompilerParams(dimension_semantics=sem, vmem_limit_bytes=VMEM_LIMIT_BYTES)


def _rms(x, gain):
    ms = jnp.mean(x * x, axis=-1, keepdims=True)
    return x * lax.rsqrt(ms + EPS) * gain


def _silu(x):
    return x * jax.nn.sigmoid(x)


def _dot(a, b):
    return jnp.dot(a, b, preferred_element_type=F32)


def _dot_nt(a, b):
    return lax.dot_general(a, b, (((1,), (1,)), ((), ())), preferred_element_type=F32)


def _split2(x):
    hi = x.astype(BF16)
    lo = (x - hi.astype(F32)).astype(BF16)
    return jnp.concatenate([hi, lo], axis=1)


def _split3(x):
    a = x.astype(BF16)
    r = x - a.astype(F32)
    b = r.astype(BF16)
    c = (r - b.astype(F32)).astype(BF16)
    return a, b, c


def _ffn_body(x_ref, gain_ref, wg_ref, wu_ref, wd_ref, o_ref, act_ref):
    x = x_ref[...]
    h = _rms(x, gain_ref[...]).astype(BF16)
    for c in range(D_FF // FFN_CHUNK):
        cs = pl.ds(c * FFN_CHUNK, FFN_CHUNK)
        g = _dot(h, wg_ref[:, cs])
        u = _dot(h, wu_ref[:, cs])
        act_ref[:, cs] = (_silu(g) * u).astype(BF16)
    y = _dot(act_ref[...], wd_ref[...])
    o_ref[...] = x + 0.5 * y


def _ffn(x2d, gain, wg, wu, wd):
    t = x2d.shape[0]
    assert t % FFN_ROWS == 0
    row = pl.BlockSpec((FFN_ROWS, D_MODEL), lambda i: (i, 0))
    return pl.pallas_call(
        _ffn_body,
        grid=(t // FFN_ROWS,),
        in_specs=[row, _const_spec((1, D_MODEL)), _const_spec((D_MODEL, D_FF)),
                  _const_spec((D_MODEL, D_FF)), _const_spec((D_FF, D_MODEL))],
        out_specs=row,
        out_shape=jax.ShapeDtypeStruct((t, D_MODEL), F32),
        scratch_shapes=[pltpu.VMEM((FFN_ROWS, D_FF), BF16)],
        compiler_params=_params("parallel"),
        name="ffn",
    )(x2d, gain, wg, wu, wd)


def _run_interleaved(*item_lists):
    keyed = [((i + 0.5) / len(items), rank, i, item)
             for rank, items in enumerate(item_lists) for i, item in enumerate(items)]
    pending = None
    for _, _, _, (start, finish, *args) in sorted(keyed, key=lambda k: k[:3]):
        y = start(*args)
        if pending is not None:
            pending[0](*pending[1], pending[2])
        pending = (finish, args, y)
    pending[0](*pending[1], pending[2])


def _proj_attn_body(x_ref, gain_ref, w_ref, qgain_ref, kgain_ref, headmean_ref,
                    qkv0_ref, qkv1_ref, qkv2_ref, slab_ref, h_ref):
    gain = gain_ref[...]

    def normed(rows):
        return _rms(rows, gain).astype(BF16)

    h_ref[0] = normed(x_ref[...])
    for j in range(D_MODEL // LANES):
        slab_ref[j] = x_ref[:, pl.ds(j * LANES, LANES)]
    for g, (_, dilation) in enumerate(ATTN_PATTERNS):
        if dilation > 1:
            per_stream = PROJ_ROWS // dilation
            h_ref[g] = normed(jnp.concatenate(
                [jnp.concatenate([slab_ref[j, pl.ds(r, per_stream, stride=dilation), :]
                                  for j in range(D_MODEL // LANES)], axis=1)
                 for r in range(dilation)], axis=0))

    def start(g, part, sub):
        col0 = part * ATTN_OUT + sub * PROJ_CHUNK
        return _dot(h_ref[g], w_ref[:, pl.ds(g * PATTERN_COLS + col0, PROJ_CHUNK)])

    def finish(g, part, sub, y):
        dilation = ATTN_PATTERNS[g][1]
        out_ref = (qkv0_ref, qkv1_ref, qkv2_ref)[g]
        per_stream = PROJ_ROWS // dilation
        if part < 2:
            ms = _dot((y * y).astype(BF16), headmean_ref[...])
            y = y * lax.rsqrt(ms + EPS) * (qgain_ref[...] if part == 0 else kgain_ref[...])
        y = y.astype(BF16)
        cols = pl.ds(part * ATTN_OUT + sub * PROJ_CHUNK, PROJ_CHUNK)
        if dilation == 1:
            out_ref[:, cols] = y
        else:
            for r in range(dilation):
                out_ref[r, :, cols] = y[r * per_stream:(r + 1) * per_stream]

    subs = ATTN_OUT // PROJ_CHUNK
    n_pat = len(ATTN_PATTERNS)
    _run_interleaved([(start, finish, g, 2, sub) for g in range(n_pat) for sub in range(subs)],
                     [(start, finish, g, part, sub)
                      for g in range(n_pat) for part in range(2) for sub in range(subs)])


def _proj_ssd_body(x_ref, gain_ref, w_ref, convw_ref, convb_ref,
                   z_ref, xbc_ref, dt_ref, gate_ref, h_ref, *ext_refs):
    @pl.when(pl.program_id(1) == 0)
    def _():
        for ext_ref in ext_refs:
            ext_ref[pl.ds(0, SUBLANES), :] = jnp.zeros((SUBLANES, PROJ_CHUNK), F32)

    h_ref[...] = _rms(x_ref[...], gain_ref[...]).astype(BF16)

    def matmul(off, c):
        return _dot(h_ref[...], w_ref[:, pl.ds(off + c * PROJ_CHUNK, PROJ_CHUNK)])

    def z_finish(off, c, y):
        z_ref[:, pl.ds(c * PROJ_CHUNK, PROJ_CHUNK)] = _silu(y).astype(BF16)

    def xbc_finish(off, c, y):
        cs = pl.ds(c * PROJ_CHUNK, PROJ_CHUNK)
        ext_ref = ext_refs[c]
        ext_ref[pl.ds(SUBLANES, PROJ_ROWS), :] = y
        acc = convb_ref[:, cs]
        for back in range(SSD_CONV):
            acc = acc + convw_ref[pl.ds(SSD_CONV - 1 - back, 1), cs] * ext_ref[pl.ds(SUBLANES - back, PROJ_ROWS), :]
        xbc_ref[:, cs] = _silu(acc).astype(BF16)
        ext_ref[pl.ds(0, SUBLANES), :] = ext_ref[pl.ds(PROJ_ROWS, SUBLANES), :]

    def gate_finish(off, c, y):
        gate_ref[:, pl.ds(c * PROJ_CHUNK, PROJ_CHUNK)] = y.astype(BF16)

    def dt_start():
        return _dot(h_ref[...], w_ref[:, pl.ds(OFF_DT, DT_COLS)])

    def dt_finish(y):
        dt_ref[...] = y

    _run_interleaved([(matmul, gate_finish, OFF_GATE, c) for c in range(2 * D_MODEL // PROJ_CHUNK)]
                     + [(dt_start, dt_finish)],
                     [(matmul, xbc_finish, OFF_XBC, c) for c in range(SSD_CONV_DIM // PROJ_CHUNK)],
                     [(matmul, z_finish, OFF_Z, c) for c in range(SSD_INNER // PROJ_CHUNK)])


def _proj(x2d, gain, w_attn, w_ssd, qgain, kgain, headmean, convw, convb, batch, seq):
    assert seq % PROJ_ROWS == 0
    tiles = seq // PROJ_ROWS
    t = batch * seq

    def row(cols):
        return pl.BlockSpec((PROJ_ROWS, cols), lambda b, i: (b * tiles + i, 0))

    def streams(dilation):
        return pl.BlockSpec((None, dilation, PROJ_ROWS // dilation, PATTERN_COLS), lambda b, i: (b, 0, i, 0))

    def stream_shape(dilation):
        return jax.ShapeDtypeStruct((batch, dilation, seq // dilation, PATTERN_COLS), BF16)

    d1, d2 = ATTN_PATTERNS[1][1], ATTN_PATTERNS[2][1]
    qkv = pl.pallas_call(
        _proj_attn_body,
        grid=(batch, tiles),
        in_specs=[row(D_MODEL), _const_spec((1, D_MODEL)), _const_spec((D_MODEL, ATTN_PROJ_COLS)),
                  _const_spec((1, PROJ_CHUNK)), _const_spec((1, PROJ_CHUNK)),
                  _const_spec((PROJ_CHUNK, PROJ_CHUNK))],
        out_specs=[row(PATTERN_COLS), streams(d1), streams(d2)],
        out_shape=[jax.ShapeDtypeStruct((t, PATTERN_COLS), BF16), stream_shape(d1), stream_shape(d2)],
        scratch_shapes=[pltpu.VMEM((D_MODEL // LANES, PROJ_ROWS, LANES), F32),
                        pltpu.VMEM((len(ATTN_PATTERNS), PROJ_ROWS, D_MODEL), BF16)],
        compiler_params=_params("parallel", "parallel"),
        name="proj_attn",
    )(x2d, gain, w_attn, qgain, kgain, headmean)
    ssd_in = pl.pallas_call(
        _proj_ssd_body,
        grid=(batch, tiles),
        in_specs=[row(D_MODEL), _const_spec((1, D_MODEL)), _const_spec((D_MODEL, SSD_PROJ_COLS)),
                  _const_spec(convw.shape), _const_spec(convb.shape)],
        out_specs=[row(SSD_INNER), row(SSD_CONV_DIM), row(DT_COLS), row(2 * D_MODEL)],
        out_shape=[jax.ShapeDtypeStruct((t, SSD_INNER), BF16),
                   jax.ShapeDtypeStruct((t, SSD_CONV_DIM), BF16),
                   jax.ShapeDtypeStruct((t, DT_COLS), F32),
                   jax.ShapeDtypeStruct((t, 2 * D_MODEL), BF16)],
        scratch_shapes=[pltpu.VMEM((PROJ_ROWS, D_MODEL), BF16)]
                       + [pltpu.VMEM((SUBLANES + PROJ_ROWS, PROJ_CHUNK), F32)] * (SSD_CONV_DIM // PROJ_CHUNK),
        compiler_params=_params("parallel", "arbitrary"),
        name="proj_ssd",
    )(x2d, gain, w_ssd, convw, convb)
    return (*qkv, *ssd_in)


def _attn_body(q_ref, kc_ref, kp_ref, vc_ref, vp_ref, bias_ref, ones_ref, o_ref, lse_ref):
    block_rows = q_ref.shape[0]
    first = jnp.where(pl.program_id(2) == 0, 1, 0)
    lane = lax.broadcasted_iota(jnp.int32, (ATTN_BLOCK, LANES), 1)
    even = lane < HEAD_DIM
    lane2 = lax.broadcasted_iota(jnp.int32, (2 * ATTN_BLOCK, LANES), 1)
    even2 = lane2 < HEAD_DIM
    zero = jnp.zeros((ATTN_BLOCK, LANES), BF16)
    zero2 = jnp.zeros((2 * ATTN_BLOCK, LANES), BF16)
    for i in range(block_rows // ATTN_BLOCK):
        rows = pl.ds(i * ATTN_BLOCK, ATTN_BLOCK)
        table = first if i == 0 else 0
        lse_tile = jnp.zeros((ATTN_BLOCK, LANES), F32)
        for p in range(HEADS_PER_PATTERN // 2):
            cols = pl.ds(p * LANES, LANES)
            q2 = q_ref[rows, cols]
            if i == 0:
                kk = jnp.concatenate([kp_ref[:, cols], kc_ref[rows, cols]], axis=0)
                vv = jnp.concatenate([vp_ref[:, cols], vc_ref[rows, cols]], axis=0)
            else:
                both = pl.ds((i - 1) * ATTN_BLOCK, 2 * ATTN_BLOCK)
                kk, vv = kc_ref[both, cols], vc_ref[both, cols]
            qq = jnp.concatenate([jnp.where(even, q2, zero), jnp.where(even, zero, q2)], axis=0)
            s = _dot_nt(qq, kk)
            probs, m_pair = [], []
            for hh in range(2):
                sh = s[hh * ATTN_BLOCK:(hh + 1) * ATTN_BLOCK] + bias_ref[table, 2 * p + hh]
                m = jnp.max(jnp.maximum(sh[:, :ATTN_BLOCK], sh[:, ATTN_BLOCK:]), axis=-1, keepdims=True)
                probs.append(jnp.exp2(sh - m).astype(BF16))
                m_pair.append(m)
            vmat = jnp.concatenate([jnp.where(even2, vv, zero2), jnp.where(even2, zero2, vv)], axis=0)
            o_ext = _dot(jnp.concatenate(probs, axis=1), jnp.concatenate([vmat, ones_ref[...]], axis=1))
            l_pair = o_ext[:, LANES:]
            o_ref[rows, cols] = (o_ext[:, :LANES] / l_pair).astype(BF16)
            lse_pair = (jnp.where(even, m_pair[0], m_pair[1]) + jnp.log2(l_pair)) * LN2
            lse_tile = jnp.where((lane == p) | (lane == HEAD_DIM + p), lse_pair, lse_tile)
        lse_ref[rows, :] = lse_tile


def _attn_pattern(qkv, bias, ones):
    batch, dilation, stream_len, _ = qkv.shape
    block_rows = min(ATTN_ROWS, stream_len)
    assert stream_len % block_rows == 0
    sub = block_rows // ATTN_BLOCK

    def cur(part):
        return pl.BlockSpec((None, None, block_rows, ATTN_OUT), lambda b, r, n: (b, r, n, part))

    def prev(part):
        return pl.BlockSpec((None, None, ATTN_BLOCK, ATTN_OUT),
                            lambda b, r, n: (b, r, jnp.maximum(n * sub - 1, 0), part))

    return pl.pallas_call(
        _attn_body,
        grid=(batch, dilation, stream_len // block_rows),
        in_specs=[cur(0), cur(1), prev(1), cur(2), prev(2), _const_spec(bias.shape), _const_spec(ones.shape)],
        out_specs=[pl.BlockSpec((None, None, block_rows, ATTN_OUT), lambda b, r, n: (b, r, n, 0)),
                   pl.BlockSpec((None, None, block_rows, LANES), lambda b, r, n: (b, r, n, 0))],
        out_shape=[jax.ShapeDtypeStruct((batch, dilation, stream_len, ATTN_OUT), BF16),
                   jax.ShapeDtypeStruct((batch, dilation, stream_len, LANES), F32)],
        compiler_params=_params("parallel", "parallel", "parallel"),
        name=f"attn_d{dilation}",
    )(qkv, qkv, qkv, qkv, qkv, bias, ones)


def _attn_bias(group, dilation):
    h = np.arange(group * HEADS_PER_PATTERN, (group + 1) * HEADS_PER_PATTERN, dtype=np.float64)
    slopes = np.exp2(-ALIBI_MAX_EXP * (h + 1) / N_ATTN_HEADS).astype(np.float32)
    a = np.arange(ATTN_BLOCK)[:, None]
    c = np.arange(ATTN_BLOCK)[None, :]
    rel_cur = (a - c).astype(np.float32)
    rel_prev = rel_cur + ATTN_BLOCK
    coef = -slopes[:, None, None] * np.float32(LOG2E)
    bias_c = np.where((a >= c)[None], coef * (rel_cur * dilation)[None], NEG_BIG)
    bias_p = np.where((c >= a)[None], coef * (rel_prev * dilation)[None], NEG_BIG)
    table = np.stack([np.concatenate([bias_p, bias_c], axis=2),
                      np.concatenate([np.full_like(bias_p, NEG_BIG), bias_c], axis=2)])
    return jnp.asarray(table.astype(np.float32))


def _attn_ones():
    ones = np.zeros((4 * ATTN_BLOCK, LANES), np.float32)
    ones[:2 * ATTN_BLOCK, :HEAD_DIM] = 1.0
    ones[2 * ATTN_BLOCK:, HEAD_DIM:] = 1.0
    return jnp.asarray(ones, BF16)


def _ssd_body(zs_ref, xbc_ref, dt_ref, dtbias_ref, alog_ref, dskip_ref, expand_ref,
              tril_ref, w_ref, o_ref, yn_ref, state_ref):
    @pl.when(pl.program_id(1) == 0)
    def _():
        state_ref[...] = jnp.zeros_like(state_ref)

    lane = lax.broadcasted_iota(jnp.int32, (SSD_CHUNK, LANES), 1)
    even = lane < HEAD_DIM
    row_i = lax.broadcasted_iota(jnp.int32, (SSD_CHUNK, SSD_CHUNK), 0)
    col_i = lax.broadcasted_iota(jnp.int32, (SSD_CHUNK, SSD_CHUNK), 1)
    causal_bias = jnp.where(row_i >= col_i, 0.0, NEG_BIG)
    zero_b = jnp.zeros((SSD_CHUNK, LANES), BF16)
    b_off = SSD_INNER
    c_off = SSD_INNER + SSD_GROUPS * SSD_STATE

    def chunk_step(ci, carry):
        rows = pl.ds(pl.multiple_of(ci * SSD_CHUNK, SSD_CHUNK), SSD_CHUNK)
        dt_pre = dt_ref[rows, :] + dtbias_ref[...]
        dt = jnp.maximum(dt_pre, 0.0) + jnp.log(1.0 + jnp.exp(-jnp.abs(dt_pre)))
        a = dt * (-LOG2E * jnp.exp(alog_ref[...]))
        a_cs = sum(_dot(tril_ref[...], t) for t in _split3(a))
        col_t = (a_cs - jnp.log2(dt)).T
        a_last = a_cs[SSD_CHUNK - 1:SSD_CHUNK, :]
        factors = jnp.concatenate([_split2(jnp.exp2(a_cs)), _split2(jnp.exp2(a_last - a_cs) * dt)], axis=0)
        for g in range(SSD_GROUPS):
            gcols = pl.ds(g * SSD_GROUP_COLS, SSD_GROUP_COLS)
            xb = xbc_ref[rows, gcols]
            x = xb.astype(F32)
            bm = xbc_ref[rows, pl.ds(b_off + g * SSD_STATE, SSD_STATE)]
            cm = xbc_ref[rows, pl.ds(c_off + g * SSD_STATE, SSD_STATE)]
            e = _dot(factors, expand_ref[:, gcols])
            e_out, e_state = e[:SSD_CHUNK], e[SSD_CHUNK:]
            cb = _dot_nt(cm, bm)
            state = state_ref[g]
            y = _dot(cm, state.astype(BF16)) * e_out
            state_ref[g] = (state * e_out[SSD_CHUNK - 1:SSD_CHUNK, :]
                            + _dot(bm.astype(F32).T.astype(BF16), (x * e_state).astype(BF16)))
            pairs = []
            for p in range(SSD_GROUP_COLS // LANES):
                lmats = []
                for hh in range(2):
                    h = g * SSD_GROUP_HEADS + 2 * p + hh
                    diff = a_cs[:, h:h + 1] - col_t[h:h + 1, :] + causal_bias
                    lmats.append((cb * jnp.exp2(diff)).astype(BF16))
                xp = xb[:, p * LANES:(p + 1) * LANES]
                rhs = jnp.concatenate([jnp.where(even, xp, zero_b), jnp.where(even, zero_b, xp)], axis=0)
                pairs.append(_dot(jnp.concatenate(lmats, axis=1), rhs))
            y = y + jnp.concatenate(pairs, axis=1) + x * dskip_ref[:, gcols]
            y = y * zs_ref[rows, gcols].astype(F32)
            ms = jnp.mean(y * y, axis=-1, keepdims=True)
            yn_ref[rows, gcols] = (y * lax.rsqrt(ms + EPS)).astype(BF16)
        return carry

    lax.fori_loop(0, SSD_ROWS // SSD_CHUNK, chunk_step, 0)
    o_ref[...] = _dot(yn_ref[...], w_ref[...]).astype(BF16)


def _ssd(zs, xbc, dt, dtbias, alog, dskip, expand, tril, w, batch, seq):
    assert seq % SSD_ROWS == 0
    tiles = seq // SSD_ROWS

    def row(cols):
        return pl.BlockSpec((SSD_ROWS, cols), lambda b, s: (b * tiles + s, 0))

    consts = [dtbias, alog, dskip, expand, tril, w]
    return pl.pallas_call(
        _ssd_body,
        grid=(batch, tiles),
        in_specs=[row(SSD_INNER), row(SSD_CONV_DIM), row(DT_COLS)] + [_const_spec(c.shape) for c in consts],
        out_specs=row(D_MODEL),
        out_shape=jax.ShapeDtypeStruct((batch * seq, D_MODEL), BF16),
        scratch_shapes=[
            pltpu.VMEM((SSD_ROWS, SSD_INNER), BF16),
            pltpu.VMEM((SSD_GROUPS, SSD_STATE, SSD_GROUP_COLS), F32),
        ],
        compiler_params=_params("parallel", "arbitrary"),
        name="ssd",
    )(zs, xbc, dt, *consts)


def _merge_body(o0_ref, o1_ref, o2_ref, l0_ref, l1_ref, l2_ref, s_ref, gate_ref, x_ref,
                expand_ref, wa_ref, wo_ref, out_ref, o_nat_ref, l_nat_ref):
    def token_order(src_ref, dst_ref, slot):
        dilation, _, cols = src_ref.shape
        if dilation == 1:
            return src_ref[0].astype(F32)
        slabs = cols // LANES
        for r in range(dilation):
            for j in range(slabs):
                dst_ref[slot, j, pl.ds(r, MERGE_ROWS // dilation, stride=dilation), :] = (
                    src_ref[r, :, pl.ds(j * LANES, LANES)].astype(F32))
        return jnp.concatenate([dst_ref[slot, j] for j in range(slabs)], axis=1)

    lses = [token_order(l_ref, l_nat_ref, k) for k, l_ref in enumerate((l0_ref, l1_ref, l2_ref))]
    m = jnp.maximum(jnp.maximum(lses[0], lses[1]), lses[2])
    es = [jnp.exp(l - m) for l in lses]
    inv = 1.0 / (es[0] + es[1] + es[2])
    attn = None
    for k, (e, o_ref) in enumerate(zip(es, (o0_ref, o1_ref, o2_ref))):
        w = _dot(_split2(e * inv), expand_ref[...])
        term = w * token_order(o_ref, o_nat_ref, k)
        attn = term if attn is None else attn + term
    a = _dot(attn.astype(BF16), wa_ref[...])
    g_attn = gate_ref[:, pl.ds(0, D_MODEL)].astype(F32)
    g_ssd = gate_ref[:, pl.ds(D_MODEL, D_MODEL)].astype(F32)
    merged = jax.nn.sigmoid(g_attn) * a + jax.nn.sigmoid(g_ssd) * s_ref[...].astype(F32)
    out_ref[...] = x_ref[...] + _dot(merged.astype(BF16), wo_ref[...])


def _merge(os_, lses, s, gates, x1, expand, wa, wo, batch, seq):
    assert seq % MERGE_ROWS == 0
    tiles = seq // MERGE_ROWS

    def row(cols):
        return pl.BlockSpec((MERGE_ROWS, cols), lambda b, i: (b * tiles + i, 0))

    def streams(arr):
        dilation, cols = arr.shape[1], arr.shape[3]
        return pl.BlockSpec((None, dilation, MERGE_ROWS // dilation, cols), lambda b, i: (b, 0, i, 0))

    n = len(ATTN_PATTERNS)
    return pl.pallas_call(
        _merge_body,
        grid=(batch, tiles),
        in_specs=[streams(o) for o in os_] + [streams(l) for l in lses]
                 + [row(D_MODEL), row(2 * D_MODEL), row(D_MODEL),
                    _const_spec(expand.shape), _const_spec(wa.shape), _const_spec(wo.shape)],
        out_specs=row(D_MODEL),
        out_shape=jax.ShapeDtypeStruct((batch * seq, D_MODEL), F32),
        scratch_shapes=[pltpu.VMEM((n, ATTN_OUT // LANES, MERGE_ROWS, LANES), F32),
                        pltpu.VMEM((n, 1, MERGE_ROWS, LANES), F32)],
        compiler_params=_params("parallel", "parallel"),
        name="merge",
    )(*os_, *lses, s, gates, x1, expand, wa, wo)


def _head_expand(lane_heads, n_heads):
    e = np.zeros((LANES, n_heads * HEAD_DIM), np.float32)
    for i, h in enumerate(lane_heads):
        if h is not None:
            e[i, h * HEAD_DIM:(h + 1) * HEAD_DIM] = 1.0
    return jnp.asarray(np.concatenate([e, e], axis=0), BF16)


def _pad_lanes(v, width):
    return jnp.pad(v.astype(F32), (0, width - v.shape[0]))[None, :]


def kernel(x, ffn1_norm, ffn1_w_gate, ffn1_w_up, ffn1_w_down, mix_norm, w_in, q_norm, k_norm, conv_w, conv_b, dt_bias, a_log, d_skip, ssd_norm, w_attn_branch, w_ssd_branch, w_out, ffn2_norm, ffn2_w_gate, ffn2_w_up, ffn2_w_down):
    batch, seq, _ = x.shape
    depth = ffn1_norm.shape[0]
    x2d = x.reshape(batch * seq, D_MODEL)

    headmean = np.zeros((PROJ_CHUNK, PROJ_CHUNK), np.float32)
    for h in range(PROJ_CHUNK // HEAD_DIM):
        headmean[h * HEAD_DIM:(h + 1) * HEAD_DIM, h * HEAD_DIM:(h + 1) * HEAD_DIM] = 1.0 / HEAD_DIM
    headmean = jnp.asarray(headmean, BF16)
    tril = jnp.asarray(np.tril(np.ones((SSD_CHUNK, SSD_CHUNK), np.float32)), BF16)
    expand_ssd = _head_expand(list(range(SSD_HEADS)), SSD_HEADS)
    pairs = HEADS_PER_PATTERN // 2
    expand_attn = _head_expand(list(range(0, HEADS_PER_PATTERN, 2)) + [None] * (HEAD_DIM - pairs)
                               + list(range(1, HEADS_PER_PATTERN, 2)), HEADS_PER_PATTERN)
    biases = [_attn_bias(g, d) for g, (_, d) in enumerate(ATTN_PATTERNS)]
    ones = _attn_ones()

    for l in range(depth):
        x2d = _ffn(x2d, ffn1_norm[l][None, :], ffn1_w_gate[l].astype(BF16), ffn1_w_up[l].astype(BF16),
                   ffn1_w_down[l].astype(BF16))

        w = w_in[l]
        n_dt = SSD_HEADS
        qkv_end = 3 * ATTN_QKV
        dt_start = qkv_end + SSD_INNER + SSD_CONV_DIM
        w_attn = jnp.concatenate(
            [w[:, part * ATTN_QKV + g * ATTN_OUT: part * ATTN_QKV + (g + 1) * ATTN_OUT]
             for g in range(len(ATTN_PATTERNS)) for part in range(3)], axis=1).astype(BF16)
        w_ssd = jnp.concatenate(
            [w[:, qkv_end:dt_start], jnp.pad(w[:, dt_start:dt_start + n_dt], ((0, 0), (0, DT_COLS - n_dt))),
             w[:, dt_start + n_dt:]], axis=1).astype(BF16)
        reps = PROJ_CHUNK // HEAD_DIM
        qgain = jnp.tile(q_norm[l].astype(F32) * (LOG2E / math.sqrt(HEAD_DIM)), reps)[None, :]
        kgain = jnp.tile(k_norm[l].astype(F32), reps)[None, :]
        qkv0, qkv1, qkv2, zs, xbc, dt, gates = _proj(
            x2d, mix_norm[l][None, :], w_attn, w_ssd, qgain, kgain, headmean,
            conv_w[l].astype(F32), conv_b[l].astype(F32)[None, :], batch, seq)

        os_, lses = [], []
        for g, qkv in enumerate((qkv0.reshape(batch, 1, seq, PATTERN_COLS), qkv1, qkv2)):
            o, lse = _attn_pattern(qkv, biases[g], ones)
            os_.append(o)
            lses.append(lse)

        s = _ssd(zs, xbc, dt, _pad_lanes(dt_bias[l], DT_COLS), _pad_lanes(a_log[l], DT_COLS),
                 jnp.repeat(d_skip[l].astype(F32), HEAD_DIM)[None, :], expand_ssd, tril,
                 (ssd_norm[l].astype(F32)[:, None] * w_ssd_branch[l]).astype(BF16), batch, seq)

        x2d = _merge(os_, lses, s, gates, x2d, expand_attn, w_attn_branch[l].astype(BF16),
                     w_out[l].astype(BF16), batch, seq)

        x2d = _ffn(x2d, ffn2_norm[l][None, :], ffn2_w_gate[l].astype(BF16), ffn2_w_up[l].astype(BF16),
                   ffn2_w_down[l].astype(BF16))
    return x2d.reshape(batch, seq, D_MODEL)
```
